```python
import math
import jax, jax.numpy as jnp
from jax import lax
import numpy as np

D_MODEL = 1024
BATCH = 8
SEQ = 2048
DEPTH = 4
DEC_BATCH = 32
DEC_SEQ = 4
PAST_LEN = 16384
PAGE_SIZE = 128

N_A_LAYERS = DEPTH // 2
N_B_LAYERS = DEPTH - N_A_LAYERS
GLA_HEADS = 4
GLA_DK = (D_MODEL // 2) // GLA_HEADS
GLA_DV = D_MODEL // GLA_HEADS
GLA_GATE_RANK = 16
GLA_GATE_TEMP = 16.0
GLA_CHUNK = 64
MLA_HEADS = 8
MLA_NOPE = 128
MLA_ROPE = 64
MLA_V = 128
MLA_Q_RANK = 384
MLA_KV_RANK = 256
ROPE_THETA = 10000.0
Q_BLOCK = 128
FFN_DIM = 2816
CONV_W = 3
EPS = 1e-6

kernel_name = 'yoco_gla_mla_convffn_step'


def rmsnorm(x, g):
    xf = x.astype(jnp.float32)
    y = xf * lax.rsqrt(jnp.mean(xf * xf, axis=-1, keepdims=True) + EPS)
    return (y * g.astype(jnp.float32)).astype(x.dtype)


def rope(x, pos):
    half = x.shape[-1] // 2
    inv = ROPE_THETA ** (-jnp.arange(half, dtype=jnp.float32) / half)
    ang = pos.astype(jnp.float32)[:, None] * inv[None, :]
    ang = ang.reshape((ang.shape[0],) + (1,) * (x.ndim - 3) + (half,))
    cos, sin = jnp.cos(ang), jnp.sin(ang)
    xf = x.astype(jnp.float32)
    x1, x2 = xf[..., :half], xf[..., half:]
    return jnp.concatenate([x1 * cos - x2 * sin, x2 * cos + x1 * sin], axis=-1).astype(x.dtype)


def gla_recurrence(q, k, v, lg, s0):
    B, T, H, _ = q.shape
    C = math.gcd(T, GLA_CHUNK)
    N = T // C

    def to_chunks(a):
        return a.reshape(B, N, C, H, a.shape[-1]).transpose(1, 0, 3, 2, 4).astype(jnp.float32)

    tril = jnp.tril(jnp.ones((C, C), dtype=bool))

    def step(S, inp):
        qc, kc, vc, gc = inp
        b = jnp.cumsum(gc, axis=2)
        b_last = b[:, :, -1:, :]
        qe = qc * jnp.exp(b)
        ke = kc * jnp.exp(-b)
        A = jnp.where(tril, jnp.einsum('bhtk,bhsk->bhts', qe, ke), 0.0)
        o = jnp.einsum('bhts,bhsv->bhtv', A, vc) + jnp.einsum('bhtk,bhkv->bhtv', qe, S)
        S_new = jnp.swapaxes(jnp.exp(b_last), -1, -2) * S + jnp.einsum(
            'bhsk,bhsv->bhkv', kc * jnp.exp(b_last - b), vc)
        return S_new, o

    S, o = lax.scan(step, s0.astype(jnp.float32), (to_chunks(q), to_chunks(k), to_chunks(v), to_chunks(lg)))
    o = o.transpose(1, 0, 3, 2, 4).reshape(B, T, H, v.shape[-1])
    return o.astype(v.dtype), S


def gla_mixer(h, s0, w_q, w_k, w_v, w_gk1, w_gk2, b_gk, w_r, g_norm, w_o):
    B, T, _ = h.shape
    q = (h @ w_q).reshape(B, T, GLA_HEADS, GLA_DK) * (GLA_DK ** -0.5)
    k = (h @ w_k).reshape(B, T, GLA_HEADS, GLA_DK)
    v = (h @ w_v).reshape(B, T, GLA_HEADS, GLA_DV)
    lg = jax.nn.log_sigmoid(((h @ w_gk1) @ w_gk2 + b_gk).astype(jnp.float32)) / GLA_GATE_TEMP
    lg = lg.reshape(B, T, GLA_HEADS, GLA_DK)
    o, s_new = gla_recurrence(q, k, v, lg, s0)
    o = rmsnorm(o, g_norm) * jax.nn.silu((h @ w_r).reshape(B, T, GLA_HEADS, GLA_DV))
    return o.reshape(B, T, GLA_HEADS * GLA_DV) @ w_o, s_new


def mla_attend(q_abs, q_rope, ckv, kr, q_pos, k_pos):
    B, T, H, R = q_abs.shape
    qb = math.gcd(T, Q_BLOCK)
    nb = T // qb
    scale = (MLA_NOPE + MLA_ROPE) ** -0.5

    def blocks(a):
        return jnp.moveaxis(a.reshape((B, nb, qb) + a.shape[2:]), 1, 0)

    def one_block(args):
        qa, qr, qp = args
        s = (jnp.einsum('bqhr,bkr->bhqk', qa, ckv) + jnp.einsum('bqhd,bkd->bhqk', qr, kr)).astype(jnp.float32) * scale
        s = jnp.where((k_pos[None, :] <= qp[:, None])[None, None], s, -jnp.inf)
        pr = jax.nn.softmax(s, axis=-1).astype(ckv.dtype)
        return jnp.einsum('bhqk,bkr->bqhr', pr, ckv)

    o = lax.map(one_block, (blocks(q_abs), blocks(q_rope), q_pos.reshape(nb, qb)))
    return jnp.moveaxis(o, 0, 1).reshape(B, T, H, R)


def mla_mixer(h, q_pos, k_pos, ckv_all, kr_all, w_dq, q_norm, w_uq, w_uk, w_uv, w_o):
    B, T, _ = h.shape
    cq = rmsnorm(h @ w_dq, q_norm)
    q = (cq @ w_uq).reshape(B, T, MLA_HEADS, MLA_NOPE + MLA_ROPE)
    q_nope = q[..., :MLA_NOPE]
    q_rope = rope(q[..., MLA_NOPE:], q_pos)
    q_abs = jnp.einsum('bthn,rhn->bthr', q_nope, w_uk)
    o_lat = mla_attend(q_abs, q_rope, ckv_all, kr_all, q_pos, k_pos)
    o = jnp.einsum('bthr,rhv->bthv', o_lat, w_uv).reshape(B, T, MLA_HEADS * MLA_V)
    return o @ w_o


def conv_ffn(h, buf, w_gate, w_up, conv_w, conv_b, w_down):
    T = h.shape[1]
    u = h @ w_gate
    upad = jnp.concatenate([buf.astype(u.dtype), u], axis=1)
    c = conv_b + sum(conv_w[j] * upad[:, j:j + T] for j in range(CONV_W))
    y = (jax.nn.gelu(c) * (h @ w_up)) @ w_down
    return y, upad[:, -(CONV_W - 1):]


def trunk(x, gla_s0, conv_s0, past_ckv, past_kr, p):
    B, T, _ = x.shape
    n_past = past_ckv.shape[1]
    q_pos = n_past + jnp.arange(T, dtype=jnp.int32)
    k_pos = jnp.arange(n_past + T, dtype=jnp.int32)
    gla_states, conv_bufs = [], []
    ckv_new = kr_new = ckv_all = kr_all = None
    for l in range(DEPTH):
        h = rmsnorm(x, p['norm_mix_pre'][l])
        if l < N_A_LAYERS:
            mix, s_new = gla_mixer(h, gla_s0[l], p['gla_w_q'][l], p['gla_w_k'][l], p['gla_w_v'][l],
                                   p['gla_w_gk1'][l], p['gla_w_gk2'][l], p['gla_b_gk'][l],
                                   p['gla_w_r'][l], p['gla_norm'][l], p['gla_w_o'][l])
            gla_states.append(s_new)
        else:
            if l == N_A_LAYERS:
                hk = rmsnorm(x, p['mla_kv_src_norm'])
                ckv_new = rmsnorm(hk @ p['mla_w_dkv'], p['mla_kv_norm'])
                kr_new = rope(hk @ p['mla_w_kr'], q_pos)
                ckv_all = jnp.concatenate([past_ckv.astype(ckv_new.dtype), ckv_new], axis=1)
                kr_all = jnp.concatenate([past_kr.astype(kr_new.dtype), kr_new], axis=1)
            j = l - N_A_LAYERS
            mix = mla_mixer(h, q_pos, k_pos, ckv_all, kr_all, p['mla_w_dq'][j], p['mla_q_norm'][j],
                            p['mla_w_uq'][j], p['mla_w_uk'], p['mla_w_uv'], p['mla_w_o'][j])
        x = x + rmsnorm(mix, p['norm_mix_post'][l])
        h = rmsnorm(x, p['norm_ffn_pre'][l])
        f, buf = conv_ffn(h, conv_s0[l], p['ffn_w_gate'][l], p['ffn_w_up'][l], p['ffn_conv_w'][l],
                          p['ffn_conv_b'][l], p['ffn_w_down'][l])
        conv_bufs.append(buf)
        x = x + rmsnorm(f, p['norm_ffn_post'][l])
    return x, jnp.stack(gla_states), jnp.stack(conv_bufs), ckv_new, kr_new


def setup_inputs(seed: int = 0) -> dict:
    key = jax.random.key(seed)
    ks = iter(jax.random.split(key, 48))
    f32 = jnp.float32

    def nrm(shape, fan_in):
        return jax.random.normal(next(ks), shape, f32) * (fan_in ** -0.5)

    def gain(shape):
        return 1.0 + 0.05 * jax.random.normal(next(ks), shape, f32)

    n_pages = PAST_LEN // PAGE_SIZE
    n_used = DEC_BATCH * n_pages
    n_pool = n_used + max(1, n_used // 4)
    D, F = D_MODEL, FFN_DIM
    out = {}
    out['x_prompt'] = jax.random.normal(next(ks), (BATCH, SEQ, D), f32)
    out['x_sample'] = jax.random.normal(next(ks), (DEC_BATCH, DEC_SEQ, D), f32)
    out['state_gla'] = 0.5 * jax.random.normal(next(ks), (N_A_LAYERS, DEC_BATCH, GLA_HEADS, GLA_DK, GLA_DV), f32)
    out['state_ffn_conv'] = jax.random.normal(next(ks), (DEPTH, DEC_BATCH, CONV_W - 1, F), f32)
    out['cache_ckv'] = jax.random.normal(next(ks), (n_pool, PAGE_SIZE, MLA_KV_RANK), f32)
    out['cache_krope'] = jax.random.normal(next(ks), (n_pool, PAGE_SIZE, MLA_ROPE), f32)
    out['page_table'] = jax.random.permutation(next(ks), n_pool)[:n_used].reshape(DEC_BATCH, n_pages).astype(jnp.int32)
    out['norm_mix_pre'] = gain((DEPTH, D))
    out['norm_mix_post'] = gain((DEPTH, D))
    out['norm_ffn_pre'] = gain((DEPTH, D))
    out['norm_ffn_post'] = gain((DEPTH, D))
    out['ffn_w_gate'] = nrm((DEPTH, D, F), D)
    out['ffn_w_up'] = nrm((DEPTH, D, F), D)
    out['ffn_conv_w'] = nrm((DEPTH, CONV_W, F), CONV_W)
    out['ffn_conv_b'] = 0.02 * jax.random.normal(next(ks), (DEPTH, F), f32)
    out['ffn_w_down'] = nrm((DEPTH, F, D), F)
    A = N_A_LAYERS
    out['gla_w_q'] = nrm((A, D, GLA_HEADS * GLA_DK), D)
    out['gla_w_k'] = nrm((A, D, GLA_HEADS * GLA_DK), D)
    out['gla_w_v'] = nrm((A, D, GLA_HEADS * GLA_DV), D)
    out['gla_w_gk1'] = nrm((A, D, GLA_GATE_RANK), D)
    out['gla_w_gk2'] = nrm((A, GLA_GATE_RANK, GLA_HEADS * GLA_DK), GLA_GATE_RANK)
    out['gla_b_gk'] = 0.1 * jax.random.normal(next(ks), (A, GLA_HEADS * GLA_DK), f32)
    out['gla_w_r'] = nrm((A, D, GLA_HEADS * GLA_DV), D)
    out['gla_norm'] = gain((A, GLA_DV))
    out['gla_w_o'] = nrm((A, GLA_HEADS * GLA_DV, D), GLA_HEADS * GLA_DV)
    Bn = N_B_LAYERS
    out['mla_kv_src_norm'] = gain((D,))
    out['mla_w_dkv'] = nrm((D, MLA_KV_RANK), D)
    out['mla_kv_norm'] = gain((MLA_KV_RANK,))
    out['mla_w_kr'] = nrm((D, MLA_ROPE), D)
    out['mla_w_uk'] = nrm((MLA_KV_RANK, MLA_HEADS, MLA_NOPE), MLA_KV_RANK)
    out['mla_w_uv'] = nrm((MLA_KV_RANK, MLA_HEADS, MLA_V), MLA_KV_RANK)
    out['mla_w_dq'] = nrm((Bn, D, MLA_Q_RANK), D)
    out['mla_q_norm'] = gain((Bn, MLA_Q_RANK))
    out['mla_w_uq'] = nrm((Bn, MLA_Q_RANK, MLA_HEADS * (MLA_NOPE + MLA_ROPE)), MLA_Q_RANK)
    out['mla_w_o'] = nrm((Bn, MLA_HEADS * MLA_V, D), MLA_HEADS * MLA_V)
    return out


def reference(x_prompt, x_sample, state_gla, state_ffn_conv, cache_ckv, cache_krope, page_table,
              norm_mix_pre, norm_mix_post, norm_ffn_pre, norm_ffn_post,
              ffn_w_gate, ffn_w_up, ffn_conv_w, ffn_conv_b, ffn_w_down,
              gla_w_q, gla_w_k, gla_w_v, gla_w_gk1, gla_w_gk2, gla_b_gk, gla_w_r, gla_norm, gla_w_o,
              mla_kv_src_norm, mla_w_dkv, mla_kv_norm, mla_w_kr, mla_w_uk, mla_w_uv,
              mla_w_dq, mla_q_norm, mla_w_uq, mla_w_o):
    p = dict(norm_mix_pre=norm_mix_pre, norm_mix_post=norm_mix_post, norm_ffn_pre=norm_ffn_pre,
             norm_ffn_post=norm_ffn_post, ffn_w_gate=ffn_w_gate, ffn_w_up=ffn_w_up, ffn_conv_w=ffn_conv_w,
             ffn_conv_b=ffn_conv_b, ffn_w_down=ffn_w_down, gla_w_q=gla_w_q, gla_w_k=gla_w_k, gla_w_v=gla_w_v,
             gla_w_gk1=gla_w_gk1, gla_w_gk2=gla_w_gk2, gla_b_gk=gla_b_gk, gla_w_r=gla_w_r, gla_norm=gla_norm,
             gla_w_o=gla_w_o, mla_kv_src_norm=mla_kv_src_norm, mla_w_dkv=mla_w_dkv, mla_kv_norm=mla_kv_norm,
             mla_w_kr=mla_w_kr, mla_w_uk=mla_w_uk, mla_w_uv=mla_w_uv, mla_w_dq=mla_w_dq, mla_q_norm=mla_q_norm,
             mla_w_uq=mla_w_uq, mla_w_o=mla_w_o)
    bp = x_prompt.shape[0]
    gla0_p = jnp.zeros((N_A_LAYERS, bp, GLA_HEADS, GLA_DK, GLA_DV), jnp.float32)
    conv0_p = jnp.zeros((DEPTH, bp, CONV_W - 1, FFN_DIM), x_prompt.dtype)
    ckv0_p = jnp.zeros((bp, 0, MLA_KV_RANK), x_prompt.dtype)
    kr0_p = jnp.zeros((bp, 0, MLA_ROPE), x_prompt.dtype)
    y_prompt, gla_p, conv_p, ckv_p, kr_p = trunk(x_prompt, gla0_p, conv0_p, ckv0_p, kr0_p, p)
    bd = x_sample.shape[0]
    past_ckv = cache_ckv[page_table].reshape(bd, -1, MLA_KV_RANK)
    past_kr = cache_krope[page_table].reshape(bd, -1, MLA_ROPE)
    y_sample, gla_s, conv_s, ckv_s, kr_s = trunk(x_sample, state_gla, state_ffn_conv, past_ckv, past_kr, p)
    return (y_prompt, y_sample, gla_p, gla_s, conv_p, conv_s, ckv_p, ckv_s, kr_p, kr_s)
```

```python
import functools

import jax
import jax.numpy as jnp
from jax import lax
from jax.experimental import pallas as pl
from jax.experimental.pallas import tpu as pltpu

D_MODEL = 1024
DEPTH = 4
N_GLA = DEPTH // 2
GLA_H = 4
GLA_DK = 128
GLA_DV = 256
GLA_GATE_TEMP = 16.0
GLA_CHUNK = 64
MLA_H = 8
MLA_NOPE = 128
MLA_ROPE = 64
MLA_V = 128
MLA_Q_RANK = 384
MLA_KV_RANK = 256
ROPE_THETA = 10000.0
FFN_DIM = 2816
CONV_W = 3
EPS = 1e-6
LANES = 128
SUBLANES = 8
VMEM_LIMIT = 56 * 1024 * 1024

F32 = jnp.float32
BF16 = jnp.bfloat16
NT_DIMS = (((1,), (1,)), ((), ()))
TN_DIMS = (((0,), (0,)), ((), ()))


def _rms(x, g):
    return x * lax.rsqrt(jnp.mean(x * x, axis=-1, keepdims=True) + EPS) * g


def _dot(a, b):
    return jnp.dot(a, b, preferred_element_type=F32)


def _log_sigmoid(z):
    return jnp.minimum(z, 0.0) - jnp.log1p(jnp.exp(-jnp.abs(z)))


def _gelu_tanh(x):
    cdf = 0.5 * (1.0 + jnp.tanh(0.7978845608028654 * (x + 0.044715 * (x * x * x))))
    return x * cdf


def _const_spec(shape):
    nd = len(shape)
    return pl.BlockSpec(shape, lambda *_: (0,) * nd, pipeline_mode=pl.Buffered(1))


def _params(n_axes):
    return pltpu.CompilerParams(dimension_semantics=("arbitrary",) * n_axes,
                                vmem_limit_bytes=VMEM_LIMIT)


def _gla_kernel(x_ref, s0_ref, gpre_ref, wqkvr_ref, wg1_ref, wg2_ref, bgk_ref, gn_ref, wo_ref,
                gpost_ref, y_ref, s_ref, proj_ref, lg_ref, o_ref, *, tm, cp, n_valid):
    @pl.when(pl.program_id(1) == 0)
    def _():
        s_ref[...] = s0_ref[...]

    x = x_ref[0]
    h = _rms(x, gpre_ref[...]).astype(BF16)
    proj_ref[...] = _dot(h, wqkvr_ref[...])
    g1 = _dot(h, wg1_ref[...])
    z = _dot(g1.astype(BF16), wg2_ref[...]) + bgk_ref[...]
    lg = _log_sigmoid(z) / GLA_GATE_TEMP
    if n_valid < tm:
        lg = jnp.where(lax.broadcasted_iota(jnp.int32, (tm, 1), 0) < n_valid, lg, 0.0)
    lg_ref[...] = lg

    tril = (lax.broadcasted_iota(jnp.int32, (cp, cp), 0)
            >= lax.broadcasted_iota(jnp.int32, (cp, cp), 1))
    tri = tril.astype(F32)
    eye = (lax.broadcasted_iota(jnp.int32, (GLA_DK, GLA_DK), 0)
           == lax.broadcasted_iota(jnp.int32, (GLA_DK, GLA_DK), 1))
    q_off, k_off, v_off, r_off = 0, GLA_H * GLA_DK, 2 * GLA_H * GLA_DK, 2 * GLA_H * GLA_DK + GLA_H * GLA_DV

    def chunk(ci, carry):
        rows = pl.ds(pl.multiple_of(ci * cp, cp), cp)
        for hd in range(GLA_H):
            kcol = slice(hd * GLA_DK, (hd + 1) * GLA_DK)
            vcol = slice(hd * GLA_DV, (hd + 1) * GLA_DV)
            gc = lg_ref[rows, kcol]
            b = jnp.dot(tri, gc, precision=lax.Precision.HIGHEST, preferred_element_type=F32)
            bl = b[cp - 1:cp, :]
            qc = proj_ref[rows, q_off + hd * GLA_DK:q_off + (hd + 1) * GLA_DK] * (GLA_DK ** -0.5)
            kc = proj_ref[rows, k_off + hd * GLA_DK:k_off + (hd + 1) * GLA_DK]
            vc = proj_ref[rows, v_off + hd * GLA_DV:v_off + (hd + 1) * GLA_DV].astype(BF16)
            qe = (qc * jnp.exp(b)).astype(BF16)
            ke = (kc * jnp.exp(-b)).astype(BF16)
            kd = (kc * jnp.exp(bl - b)).astype(BF16)
            a = lax.dot_general(qe, ke, NT_DIMS, preferred_element_type=F32)
            a = jnp.where(tril, a, 0.0).astype(BF16)
            s = s_ref[0, hd]
            o_ref[rows, vcol] = _dot(a, vc) + _dot(qe, s.astype(BF16))
            dcol = jnp.sum(jnp.where(eye, jnp.exp(bl), 0.0), axis=1, keepdims=True)
            s_ref[0, hd] = dcol * s + lax.dot_general(kd, vc, TN_DIMS, preferred_element_type=F32)
        return carry

    lax.fori_loop(0, tm // cp, chunk, 0)

    gn = gn_ref[...]
    for hd in range(GLA_H):
        vcol = slice(hd * GLA_DV, (hd + 1) * GLA_DV)
        r = proj_ref[:, r_off + hd * GLA_DV:r_off + (hd + 1) * GLA_DV]
        o_ref[:, vcol] = _rms(o_ref[:, vcol], gn) * (r * jax.nn.sigmoid(r))
    mix = _dot(o_ref[...].astype(BF16), wo_ref[...])
    y_ref[0] = x + _rms(mix, gpost_ref[...])


def _gla_layer(x, s0, w, *, tm, cp, n_valid):
    bsz, t, d = x.shape
    assert t % tm == 0 and tm % cp == 0 and (n_valid == tm or t == tm)
    n_proj = w["wqkvr"].shape[1]
    x_spec = pl.BlockSpec((1, tm, d), lambda b, i: (b, i, 0))
    s_spec = pl.BlockSpec((1, GLA_H, GLA_DK, GLA_DV), lambda b, i: (b, 0, 0, 0))
    return pl.pallas_call(
        functools.partial(_gla_kernel, tm=tm, cp=cp, n_valid=n_valid),
        grid=(bsz, t // tm),
        in_specs=[x_spec, s_spec, _const_spec((1, d)), _const_spec(w["wqkvr"].shape),
                  _const_spec(w["wg1"].shape), _const_spec(w["wg2"].shape), _const_spec(w["bgk"].shape),
                  _const_spec(w["gnorm"].shape), _const_spec(w["wo"].shape), _const_spec((1, d))],
        out_specs=[x_spec, s_spec],
        out_shape=[jax.ShapeDtypeStruct(x.shape, F32),
                   jax.ShapeDtypeStruct((bsz, GLA_H, GLA_DK, GLA_DV), F32)],
        scratch_shapes=[pltpu.VMEM((tm, n_proj), F32), pltpu.VMEM((tm, GLA_H * GLA_DK), F32),
                        pltpu.VMEM((tm, GLA_H * GLA_DV), F32)],
        compiler_params=_params(2),
        name="gla_layer",
    )(x, s0, w["gpre"], w["wqkvr"], w["wg1"], w["wg2"], w["bgk"], w["gnorm"], w["wo"], w["gpost"])


def _ffn_kernel(x_ref, buf_ref, gpre_ref, wg_ref, wu_ref, cw_ref, cb_ref, wd_ref, gpost_ref,
                y_ref, bufn_ref, carry_ref, *, tm, n_valid, fc):
    @pl.when(pl.program_id(1) == 0)
    def _():
        carry_ref[...] = buf_ref[0]

    x = x_ref[0]
    h = _rms(x, gpre_ref[...]).astype(BF16)
    row = lax.broadcasted_iota(jnp.int32, (tm, 1), 0)
    acc = jnp.zeros((tm, D_MODEL), F32)
    for c0 in range(0, FFN_DIM, fc):
        sl = slice(c0, c0 + fc)
        u = _dot(h, wg_ref[:, sl])
        up = _dot(h, wu_ref[:, sl])
        p0 = carry_ref[0:1, sl]
        p1 = carry_ref[1:2, sl]
        u1 = jnp.where(row == 0, p1, pltpu.roll(u, 1, 0))
        u2 = jnp.where(row == 0, p0, jnp.where(row == 1, p1, pltpu.roll(u, 2, 0)))
        c = cb_ref[:, sl] + ((cw_ref[0:1, sl] * u2 + cw_ref[1:2, sl] * u1) + cw_ref[2:3, sl] * u)
        act = (_gelu_tanh(c) * up).astype(BF16)
        acc = acc + _dot(act, wd_ref[sl, :])
        tail = u[n_valid - (CONV_W - 1):n_valid, :]
        carry_ref[:, sl] = tail
        bufn_ref[0, :, sl] = tail
    y_ref[0] = x + _rms(acc, gpost_ref[...])


def _ffn_layer(x, buf, w, *, tm, n_valid):
    bsz, t, d = x.shape
    assert t % tm == 0 and (n_valid == tm or t == tm) and n_valid >= CONV_W - 1
    fc = FFN_DIM // 2
    assert fc % LANES == 0
    x_spec = pl.BlockSpec((1, tm, d), lambda b, i: (b, i, 0))
    b_spec = pl.BlockSpec((1, CONV_W - 1, FFN_DIM), lambda b, i: (b, 0, 0))
    return pl.pallas_call(
        functools.partial(_ffn_kernel, tm=tm, n_valid=n_valid, fc=fc),
        grid=(bsz, t // tm),
        in_specs=[x_spec, b_spec, _const_spec((1, d)), _const_spec((d, FFN_DIM)), _const_spec((d, FFN_DIM)),
                  _const_spec((CONV_W, FFN_DIM)), _const_spec((1, FFN_DIM)), _const_spec((FFN_DIM, d)),
                  _const_spec((1, d))],
        out_specs=[x_spec, b_spec],
        out_shape=[jax.ShapeDtypeStruct(x.shape, F32),
                   jax.ShapeDtypeStruct((bsz, CONV_W - 1, FFN_DIM), F32)],
        scratch_shapes=[pltpu.VMEM((CONV_W - 1, FFN_DIM), F32)],
        compiler_params=_params(2),
        name="ffn_layer",
    )(x, buf, w["gpre"], w["wg"], w["wu"], w["cw"], w["cb"], w["wd"], w["gpost"])


def _kv_kernel(x_ref, cos_ref, sin_ref, gsrc_ref, wdkv_ref, gkv_ref, wkr_ref, wkrot_ref, ckv_ref, kr_ref):
    hk = _rms(x_ref[0], gsrc_ref[...]).astype(BF16)
    ckv_ref[0] = _rms(_dot(hk, wdkv_ref[...]), gkv_ref[...])
    kr_ref[0] = _dot(hk, wkr_ref[...]) * cos_ref[...] + _dot(hk, wkrot_ref[...]) * sin_ref[...]


def _kv_proj(x, cos, sin, w, *, tm):
    bsz, t, d = x.shape
    assert t % tm == 0
    x_spec = pl.BlockSpec((1, tm, d), lambda b, i: (b, i, 0))
    t_spec = pl.BlockSpec((tm, MLA_ROPE), lambda b, i: (i, 0))
    return pl.pallas_call(
        _kv_kernel,
        grid=(bsz, t // tm),
        in_specs=[x_spec, t_spec, t_spec, _const_spec((1, d)), _const_spec((d, MLA_KV_RANK)),
                  _const_spec((1, MLA_KV_RANK)), _const_spec((d, MLA_ROPE)), _const_spec((d, MLA_ROPE))],
        out_specs=[pl.BlockSpec((1, tm, MLA_KV_RANK), lambda b, i: (b, i, 0)),
                   pl.BlockSpec((1, tm, MLA_ROPE), lambda b, i: (b, i, 0))],
        out_shape=[jax.ShapeDtypeStruct((bsz, t, MLA_KV_RANK), F32),
                   jax.ShapeDtypeStruct((bsz, t, MLA_ROPE), F32)],
        compiler_params=_params(2),
        name="kv_proj",
    )(x, cos, sin, w["gsrc"], w["wdkv"], w["gkv"], w["wkr"], w["wkrot"])


def _mla_queries(x, cos, sin, gpre_ref, wdq_ref, qn_ref, wuq_ref, wuk_ref, qa_ref, qr_ref, tq):
    h = _rms(x, gpre_ref[...]).astype(BF16)
    cq = _rms(_dot(h, wdq_ref[...]), qn_ref[...]).astype(BF16)
    q = _dot(cq, wuq_ref[...])
    hw = MLA_H * LANES
    for hd in range(MLA_H):
        rows = slice(hd * tq, (hd + 1) * tq)
        qn = q[:, hd * MLA_NOPE:(hd + 1) * MLA_NOPE].astype(BF16)
        qa_ref[rows, :] = lax.dot_general(qn, wuk_ref[hd], NT_DIMS, preferred_element_type=F32).astype(BF16)
        roped = q[:, hw + hd * LANES:hw + (hd + 1) * LANES] * cos + q[:, 2 * hw + hd * LANES:2 * hw + (hd + 1) * LANES] * sin
        qr_ref[rows, :] = roped[:, :MLA_ROPE].astype(BF16)


def _attend_block(qa_ref, qr_ref, m_ref, l_ref, acc_ref, kc, kk, visible):
    s = (lax.dot_general(qa_ref[...], kc, NT_DIMS, preferred_element_type=F32)
         + lax.dot_general(qr_ref[...], kk, NT_DIMS, preferred_element_type=F32))
    s = s * ((MLA_NOPE + MLA_ROPE) ** -0.5)
    if visible is not None:
        s = jnp.where(visible, s, -jnp.inf)
    m_prev = m_ref[...]
    m_new = jnp.maximum(m_prev, jnp.max(s, axis=1, keepdims=True))
    alpha = jnp.exp(m_prev - m_new)
    p = jnp.exp(s - m_new)
    l_ref[...] = alpha * l_ref[...] + jnp.sum(p, axis=1, keepdims=True)
    acc_ref[...] = alpha * acc_ref[...] + _dot(p.astype(BF16), kc)
    m_ref[...] = m_new


def _mla_output(x, acc_ref, l_ref, oc_ref, wuv_ref, wo_ref, gpost_ref, tq):
    for hd in range(MLA_H):
        rows = slice(hd * tq, (hd + 1) * tq)
        ol = (acc_ref[rows, :] / l_ref[rows, :]).astype(BF16)
        oc_ref[:, hd * MLA_V:(hd + 1) * MLA_V] = _dot(ol, wuv_ref[hd])
    mix = _dot(oc_ref[...].astype(BF16), wo_ref[...])
    return x + _rms(mix, gpost_ref[...])


def _mla_prompt_kernel(x_ref, ckv_ref, kr_ref, cos_ref, sin_ref, gpre_ref, wdq_ref, qn_ref, wuq_ref,
                       wuk_ref, wuv_ref, wo_ref, gpost_ref, y_ref,
                       qa_ref, qr_ref, m_ref, l_ref, acc_ref, oc_ref, *, tq):
    i = pl.program_id(1)
    x = x_ref[0]
    _mla_queries(x, cos_ref[...], sin_ref[...], gpre_ref, wdq_ref, qn_ref, wuq_ref, wuk_ref, qa_ref, qr_ref, tq)
    m_ref[...] = jnp.full(m_ref.shape, -jnp.inf, F32)
    l_ref[...] = jnp.zeros(l_ref.shape, F32)
    acc_ref[...] = jnp.zeros(acc_ref.shape, F32)
    qpos = i * tq + (lax.broadcasted_iota(jnp.int32, (MLA_H * tq, 1), 0) & (tq - 1))

    def body(j, carry):
        k0 = pl.multiple_of(j * tq, tq)
        kc = ckv_ref[0, pl.ds(k0, tq), :].astype(BF16)
        kk = kr_ref[0, pl.ds(k0, tq), :].astype(BF16)
        kpos = k0 + lax.broadcasted_iota(jnp.int32, (1, tq), 1)
        _attend_block(qa_ref, qr_ref, m_ref, l_ref, acc_ref, kc, kk, kpos <= qpos)
        return carry

    lax.fori_loop(0, i + 1, body, 0)
    y_ref[0] = _mla_output(x, acc_ref, l_ref, oc_ref, wuv_ref, wo_ref, gpost_ref, tq)


def _mla_scratch(tq):
    m = MLA_H * tq
    return [pltpu.VMEM((m, MLA_KV_RANK), BF16), pltpu.VMEM((m, MLA_ROPE), BF16),
            pltpu.VMEM((m, 1), F32), pltpu.VMEM((m, 1), F32), pltpu.VMEM((m, MLA_KV_RANK), F32),
            pltpu.VMEM((tq, MLA_H * MLA_V), F32)]


def _mla_weight_specs(w):
    return [_const_spec(w[k].shape) for k in ("gpre", "wdq", "qnorm", "wuq", "wuk", "wuv", "wo", "gpost")]


def _mla_weights(w):
    return [w[k] for k in ("gpre", "wdq", "qnorm", "wuq", "wuk", "wuv", "wo", "gpost")]


def _mla_prompt(x, ckv, kr, cos, sin, w, *, tq):
    bsz, t, d = x.shape
    assert t % tq == 0 and tq & (tq - 1) == 0
    x_spec = pl.BlockSpec((1, tq, d), lambda b, i: (b, i, 0))
    t_spec = pl.BlockSpec((tq, LANES), lambda b, i: (i, 0))
    return pl.pallas_call(
        functools.partial(_mla_prompt_kernel, tq=tq),
        grid=(bsz, t // tq),
        in_specs=[x_spec,
                  pl.BlockSpec((1, t, MLA_KV_RANK), lambda b, i: (b, 0, 0)),
                  pl.BlockSpec((1, t, MLA_ROPE), lambda b, i: (b, 0, 0)),
                  t_spec, t_spec] + _mla_weight_specs(w),
        out_specs=x_spec,
        out_shape=jax.ShapeDtypeStruct(x.shape, F32),
        scratch_shapes=_mla_scratch(tq),
        compiler_params=_params(2),
        name="mla_prompt",
    )(x, ckv, kr, cos, sin, *_mla_weights(w))


def _mla_sample_kernel(pt_ref, x_ref, ckvn_ref, krn_ref, cos_ref, sin_ref, gpre_ref, wdq_ref, qn_ref,
                       wuq_ref, wuk_ref, wuv_ref, wo_ref, gpost_ref, *rest, tq, n_valid, pages):
    del pt_ref
    ckv_pages = rest[:pages]
    kr_pages = rest[pages:2 * pages]
    y_ref, qa_ref, qr_ref, m_ref, l_ref, acc_ref, oc_ref = rest[2 * pages:]
    j = pl.program_id(1)

    @pl.when(j == 0)
    def _():
        _mla_queries(x_ref[0], cos_ref[...], sin_ref[...], gpre_ref, wdq_ref, qn_ref, wuq_ref, wuk_ref,
                     qa_ref, qr_ref, tq)
        m_ref[...] = jnp.full(m_ref.shape, -jnp.inf, F32)
        l_ref[...] = jnp.zeros(l_ref.shape, F32)
        acc_ref[...] = jnp.zeros(acc_ref.shape, F32)

    for i in range(pages):
        _attend_block(qa_ref, qr_ref, m_ref, l_ref, acc_ref,
                      ckv_pages[i][0].astype(BF16), kr_pages[i][0].astype(BF16), None)

    @pl.when(j == pl.num_programs(1) - 1)
    def _():
        n_new = ckvn_ref.shape[1]
        qpos = lax.broadcasted_iota(jnp.int32, (MLA_H * tq, 1), 0) & (tq - 1)
        kpos = lax.broadcasted_iota(jnp.int32, (1, n_new), 1)
        visible = (kpos <= qpos) & (kpos < n_valid)
        _attend_block(qa_ref, qr_ref, m_ref, l_ref, acc_ref,
                      ckvn_ref[0].astype(BF16), krn_ref[0].astype(BF16), visible)
        y_ref[0] = _mla_output(x_ref[0], acc_ref, l_ref, oc_ref, wuv_ref, wo_ref, gpost_ref, tq)


def _mla_sample(x, ckv_new, kr_new, cache_ckv, cache_kr, page_table, cos, sin, w, *, n_valid, pages):
    bsz, tq, d = x.shape
    n_pages = page_table.shape[1]
    page = cache_ckv.shape[1]
    assert n_pages % pages == 0 and tq & (tq - 1) == 0
    n_new = ckv_new.shape[1]
    x_spec = pl.BlockSpec((1, tq, d), lambda b, j, pt: (b, 0, 0))
    page_specs = (
        [pl.BlockSpec((1, page, MLA_KV_RANK), lambda b, j, pt, i=i: (pt[b, j * pages + i], 0, 0))
         for i in range(pages)]
        + [pl.BlockSpec((1, page, MLA_ROPE), lambda b, j, pt, i=i: (pt[b, j * pages + i], 0, 0))
           for i in range(pages)])
    grid_spec = pltpu.PrefetchScalarGridSpec(
        num_scalar_prefetch=1,
        grid=(bsz, n_pages // pages),
        in_specs=[x_spec,
                  pl.BlockSpec((1, n_new, MLA_KV_RANK), lambda b, j, pt: (b, 0, 0)),
                  pl.BlockSpec((1, n_new, MLA_ROPE), lambda b, j, pt: (b, 0, 0)),
                  _const_spec((tq, LANES)), _const_spec((tq, LANES))] + _mla_weight_specs(w) + page_specs,
        out_specs=x_spec,
        scratch_shapes=_mla_scratch(tq),
    )
    return pl.pallas_call(
        functools.partial(_mla_sample_kernel, tq=tq, n_valid=n_valid, pages=pages),
        grid_spec=grid_spec,
        out_shape=jax.ShapeDtypeStruct(x.shape, F32),
        compiler_params=_params(2),
        name="mla_sample",
    )(page_table, x, ckv_new, kr_new, cos, sin, *_mla_weights(w),
      *([cache_ckv] * pages), *([cache_kr] * pages))


def _row(v):
    return v.reshape(1, -1).astype(F32)


def _rot_cols(w):
    half = w.shape[-1] // 2
    return jnp.concatenate([-w[..., half:], w[..., :half]], axis=-1)


def _prep_weights(p):
    gla, ffn, mla = [], [], []
    for l in range(N_GLA):
        rank = p["gla_w_gk1"].shape[-1]
        gla.append(dict(
            gpre=_row(p["norm_mix_pre"][l]), gpost=_row(p["norm_mix_post"][l]),
            wqkvr=jnp.concatenate([p["gla_w_q"][l], p["gla_w_k"][l], p["gla_w_v"][l], p["gla_w_r"][l]],
                                  axis=1).astype(BF16),
            wg1=jnp.pad(p["gla_w_gk1"][l], ((0, 0), (0, LANES - rank))).astype(BF16),
            wg2=jnp.pad(p["gla_w_gk2"][l], ((0, LANES - rank), (0, 0))).astype(BF16),
            bgk=_row(p["gla_b_gk"][l]), gnorm=_row(p["gla_norm"][l]), wo=p["gla_w_o"][l].astype(BF16)))
    for l in range(DEPTH):
        ffn.append(dict(
            gpre=_row(p["norm_ffn_pre"][l]), gpost=_row(p["norm_ffn_post"][l]),
            wg=p["ffn_w_gate"][l].astype(BF16), wu=p["ffn_w_up"][l].astype(BF16),
            cw=p["ffn_conv_w"][l].astype(F32), cb=_row(p["ffn_conv_b"][l]), wd=p["ffn_w_down"][l].astype(BF16)))
    wuk = jnp.transpose(p["mla_w_uk"], (1, 0, 2)).astype(BF16)
    wuv = jnp.transpose(p["mla_w_uv"], (1, 0, 2)).astype(BF16)
    for j in range(DEPTH - N_GLA):
        l = N_GLA + j
        wuq = p["mla_w_uq"][j].reshape(MLA_Q_RANK, MLA_H, MLA_NOPE + MLA_ROPE)
        nope = wuq[:, :, :MLA_NOPE].reshape(MLA_Q_RANK, MLA_H * MLA_NOPE)
        rope = wuq[:, :, MLA_NOPE:]
        lane_pad = ((0, 0), (0, 0), (0, LANES - MLA_ROPE))
        rope_p = jnp.pad(rope, lane_pad).reshape(MLA_Q_RANK, MLA_H * LANES)
        rot_p = jnp.pad(_rot_cols(rope), lane_pad).reshape(MLA_Q_RANK, MLA_H * LANES)
        mla.append(dict(
            gpre=_row(p["norm_mix_pre"][l]), gpost=_row(p["norm_mix_post"][l]),
            wdq=p["mla_w_dq"][j].astype(BF16), qnorm=_row(p["mla_q_norm"][j]),
            wuq=jnp.concatenate([nope, rope_p, rot_p], axis=1).astype(BF16),
            wuk=wuk, wuv=wuv, wo=p["mla_w_o"][j].astype(BF16)))
    kv = dict(gsrc=_row(p["mla_kv_src_norm"]), wdkv=p["mla_w_dkv"].astype(BF16), gkv=_row(p["mla_kv_norm"]),
              wkr=p["mla_w_kr"].astype(BF16), wkrot=_rot_cols(p["mla_w_kr"]).astype(BF16))
    return gla, ffn, mla, kv


def _rope_tables(pos):
    half = MLA_ROPE // 2
    inv = ROPE_THETA ** (-jnp.arange(half, dtype=F32) / half)
    ang = pos.astype(F32)[:, None] * inv[None, :]
    cos, sin = jnp.cos(ang), jnp.sin(ang)
    return jnp.concatenate([cos, cos], axis=-1), jnp.concatenate([sin, sin], axis=-1)


def _trunk_prompt(x, weights):
    gla_w, ffn_w, mla_w, kv_w = weights
    bsz, t, _ = x.shape
    cos, sin = _rope_tables(jnp.arange(t, dtype=jnp.int32))
    cos2, sin2 = jnp.concatenate([cos, cos], -1), jnp.concatenate([sin, sin], -1)
    s0 = jnp.zeros((bsz, GLA_H, GLA_DK, GLA_DV), F32)
    buf0 = jnp.zeros((bsz, CONV_W - 1, FFN_DIM), F32)
    states, bufs = [], []
    ckv = kr = None
    for l in range(DEPTH):
        if l < N_GLA:
            x, s = _gla_layer(x, s0, gla_w[l], tm=256, cp=GLA_CHUNK, n_valid=256)
            states.append(s)
        else:
            if l == N_GLA:
                ckv, kr = _kv_proj(x, cos, sin, kv_w, tm=512)
            x = _mla_prompt(x, ckv, kr, cos2, sin2, mla_w[l - N_GLA], tq=256)
        x, b = _ffn_layer(x, buf0, ffn_w[l], tm=512, n_valid=512)
        bufs.append(b)
    return x, jnp.stack(states), jnp.stack(bufs), ckv, kr


def _trunk_sample(x, state_gla, state_conv, cache_ckv, cache_kr, page_table, weights):
    gla_w, ffn_w, mla_w, kv_w = weights
    bsz, t, _ = x.shape
    tp = -(-t // SUBLANES) * SUBLANES
    assert tp & (tp - 1) == 0
    page = cache_ckv.shape[1]
    past = page_table.shape[1] * page
    x = jnp.pad(x, ((0, 0), (0, tp - t), (0, 0)))
    cos, sin = _rope_tables(past + jnp.arange(tp, dtype=jnp.int32))
    cos2, sin2 = jnp.concatenate([cos, cos], -1), jnp.concatenate([sin, sin], -1)
    states, bufs = [], []
    ckv = kr = ckv_pad = kr_pad = None
    for l in range(DEPTH):
        if l < N_GLA:
            x, s = _gla_layer(x, state_gla[l], gla_w[l], tm=tp, cp=tp, n_valid=t)
            states.append(s)
        else:
            if l == N_GLA:
                ckv, kr = _kv_proj(x, cos, sin, kv_w, tm=tp)
                ckv_pad = jnp.pad(ckv, ((0, 0), (0, page - tp), (0, 0)))
                kr_pad = jnp.pad(kr, ((0, 0), (0, page - tp), (0, 0)))
            x = _mla_sample(x, ckv_pad, kr_pad, cache_ckv, cache_kr, page_table, cos2, sin2,
                            mla_w[l - N_GLA], n_valid=t, pages=8)
        x, b = _ffn_layer(x, state_conv[l], ffn_w[l], tm=tp, n_valid=t)
        bufs.append(b)
    return x[:, :t], jnp.stack(states), jnp.stack(bufs), ckv[:, :t], kr[:, :t]


def kernel(x_prompt, x_sample, state_gla, state_ffn_conv, cache_ckv, cache_krope, page_table, norm_mix_pre, norm_mix_post, norm_ffn_pre, norm_ffn_post, ffn_w_gate, ffn_w_up, ffn_conv_w, ffn_conv_b, ffn_w_down, gla_w_q, gla_w_k, gla_w_v, gla_w_gk1, gla_w_gk2, gla_b_gk, gla_w_r, gla_norm, gla_w_o, mla_kv_src_norm, mla_w_dkv, mla_kv_norm, mla_w_kr, mla_w_uk, mla_w_uv, mla_w_dq, mla_q_norm, mla_w_uq, mla_w_o):
    p = dict(norm_mix_pre=norm_mix_pre, norm_mix_post=norm_mix_post, norm_ffn_pre=norm_ffn_pre,
             norm_ffn_post=norm_ffn_post, ffn_w_gate=ffn_w_gate, ffn_w_up=ffn_w_up, ffn_conv_w=ffn_conv_w,
             ffn_conv_b=ffn_conv_b, ffn_w_down=ffn_w_down, gla_w_q=gla_w_q, gla_w_k=gla_w_k, gla_w_v=gla_w_v,
             gla_w_gk1=gla_w_gk1, gla_w_gk2=gla_w_gk2, gla_b_gk=gla_b_gk, gla_w_r=gla_w_r, gla_norm=gla_norm,
             gla_w_o=gla_w_o, mla_kv_src_norm=mla_kv_src_norm, mla_w_dkv=mla_w_dkv, mla_kv_norm=mla_kv_norm,
             mla_w_kr=mla_w_kr, mla_w_uk=mla_w_uk, mla_w_uv=mla_w_uv, mla_w_dq=mla_w_dq, mla_q_norm=mla_q_norm,
             mla_w_uq=mla_w_uq, mla_w_o=mla_w_o)
    weights = _prep_weights(p)
    y_p, gla_p, conv_p, ckv_p, kr_p = _trunk_prompt(x_prompt, weights)
    y_s, gla_s, conv_s, ckv_s, kr_s = _trunk_sample(x_sample, state_gla, state_ffn_conv, cache_ckv,
                                                    cache_krope, page_table, weights)
    return (y_p, y_s, gla_p, gla_s, conv_p, conv_s, ckv_p, ckv_s, kr_p, kr_s)
```

```python
import functools

import jax
import jax.numpy as jnp
from jax import lax
from jax.experimental import pallas as pl
from jax.experimental.pallas import tpu as pltpu

D_MODEL = 1024
DEPTH = 4
N_GLA = DEPTH // 2
GLA_H = 4
GLA_DK = 128
GLA_DV = 256
GLA_GATE_TEMP = 16.0
GLA_CHUNK = 64
MLA_H = 8
MLA_NOPE = 128
MLA_ROPE = 64
MLA_V = 128
MLA_Q_RANK = 384
MLA_KV_RANK = 256
ROPE_THETA = 10000.0
FFN_DIM = 2816
CONV_W = 3
EPS = 1e-6
LANES = 128
SUBLANES = 8
VMEM_LIMIT = 56 * 1024 * 1024

F32 = jnp.float32
BF16 = jnp.bfloat16
NT_DIMS = (((1,), (1,)), ((), ()))
TN_DIMS = (((0,), (0,)), ((), ()))
LOG2_E = 1.4426950408889634
SM_SCALE_LOG2 = (MLA_NOPE + MLA_ROPE) ** -0.5 * LOG2_E


def _rms(x, g):
    return x * lax.rsqrt(jnp.mean(x * x, axis=-1, keepdims=True) + EPS) * g


def _dot(a, b):
    return jnp.dot(a, b, preferred_element_type=F32)


def _dot_nt(a, b):
    return lax.dot_general(a, b, NT_DIMS, preferred_element_type=F32)


def _log_sigmoid(z):
    return jnp.minimum(z, 0.0) - jnp.log1p(jnp.exp(-jnp.abs(z)))


def _gelu_tanh(x):
    cdf = 0.5 * (1.0 + jnp.tanh(0.7978845608028654 * (x + 0.044715 * (x * x * x))))
    return x * cdf


def _const_spec(shape):
    nd = len(shape)
    return pl.BlockSpec(shape, lambda *_: (0,) * nd, pipeline_mode=pl.Buffered(1))


def _params(n_axes):
    return pltpu.CompilerParams(dimension_semantics=("arbitrary",) * n_axes,
                                vmem_limit_bytes=VMEM_LIMIT)


def _gla_kernel(x_ref, s0_ref, gpre_ref, wqkvr_ref, wg1_ref, wg2_ref, bgk_ref, gn_ref, wo_ref,
                gpost_ref, y_ref, s_ref, proj_ref, lg_ref, o_ref, *, tm, cp, n_valid, tiles_per_state):
    if tiles_per_state:
        @pl.when(pl.program_id(0) % tiles_per_state == 0)
        def _():
            s_ref[...] = s0_ref[...]
    else:
        s_ref[...] = s0_ref[...]

    x = x_ref[...]
    h = _rms(x, gpre_ref[...]).astype(BF16)
    proj_ref[...] = _dot(h, wqkvr_ref[...])
    g1 = _dot(h, wg1_ref[...])
    z = _dot(g1.astype(BF16), wg2_ref[...]) + bgk_ref[...]
    lg = _log_sigmoid(z) / GLA_GATE_TEMP
    if n_valid < cp:
        t_in_chunk = lax.broadcasted_iota(jnp.int32, (tm, 1), 0) & (cp - 1)
        lg = jnp.where(t_in_chunk < n_valid, lg, 0.0)
    lg_ref[...] = lg

    nc = tm // cp
    r_i = lax.broadcasted_iota(jnp.int32, (tm, tm), 0)
    c_i = lax.broadcasted_iota(jnp.int32, (tm, tm), 1)
    intra = (r_i >= c_i) & ((r_i & -cp) == (c_i & -cp))
    t_in = lax.broadcasted_iota(jnp.int32, (tm, GLA_DK), 0) & (cp - 1)
    eye = (lax.broadcasted_iota(jnp.int32, (GLA_DK, GLA_DK), 0)
           == lax.broadcasted_iota(jnp.int32, (GLA_DK, GLA_DK), 1))
    k_off = GLA_H * GLA_DK
    v_off = 2 * GLA_H * GLA_DK
    r_off = v_off + GLA_H * GLA_DV

    for hd in range(GLA_H):
        kcol = slice(hd * GLA_DK, (hd + 1) * GLA_DK)
        vcol = slice(hd * GLA_DV, (hd + 1) * GLA_DV)
        b = lg_ref[:, kcol]
        shift = 1
        while shift < cp:
            b = b + jnp.where(t_in >= shift, pltpu.roll(b, shift, 0), 0.0)
            shift *= 2
        bl = jnp.concatenate(
            [jnp.broadcast_to(b[(c + 1) * cp - 1:(c + 1) * cp, :], (cp, GLA_DK)) for c in range(nc)], axis=0)
        q = proj_ref[:, hd * GLA_DK:(hd + 1) * GLA_DK] * (GLA_DK ** -0.5)
        k = proj_ref[:, k_off + hd * GLA_DK:k_off + (hd + 1) * GLA_DK]
        v = proj_ref[:, v_off + hd * GLA_DV:v_off + (hd + 1) * GLA_DV]
        qe = q * jnp.exp(b)
        ke = k * jnp.exp(-b)
        kd = k * jnp.exp(bl - b)
        a = jnp.where(intra, _dot_nt(qe.astype(BF16), ke.astype(BF16)), 0.0).astype(BF16)
        o_intra = _dot(a, v.astype(BF16))
        s = s_ref[0, hd]
        for c in range(nc):
            rows = slice(c * cp, (c + 1) * cp)
            if not tiles_per_state:
                s = s_ref[c, hd]
            o_ref[rows, vcol] = o_intra[rows, :] + _dot(qe[rows, :].astype(BF16), s.astype(BF16))
            dcol = jnp.sum(jnp.where(eye, jnp.exp(bl[c * cp:c * cp + 1, :]), 0.0), axis=1, keepdims=True)
            s = dcol * s + lax.dot_general(kd[rows, :].astype(BF16), v[rows, :].astype(BF16), TN_DIMS,
                                           preferred_element_type=F32)
            if not tiles_per_state:
                s_ref[c, hd] = s
        if tiles_per_state:
            s_ref[0, hd] = s

    gn = gn_ref[...]
    for hd in range(GLA_H):
        vcol = slice(hd * GLA_DV, (hd + 1) * GLA_DV)
        r = proj_ref[:, r_off + hd * GLA_DV:r_off + (hd + 1) * GLA_DV]
        o_ref[:, vcol] = _rms(o_ref[:, vcol], gn) * (r * jax.nn.sigmoid(r))
    mix = _dot(o_ref[...].astype(BF16), wo_ref[...])
    y_ref[...] = x + _rms(mix, gpost_ref[...])


def _gla_layer(x, s0, w, *, tm, cp, n_valid, tiles_per_state):
    rows, d = x.shape
    assert rows % tm == 0 and tm % cp == 0 and cp & (cp - 1) == 0
    n_proj = w["wqkvr"].shape[1]
    x_spec = pl.BlockSpec((tm, d), lambda i: (i, 0))
    if tiles_per_state:
        s_spec = pl.BlockSpec((1, GLA_H, GLA_DK, GLA_DV), lambda i: (i // tiles_per_state, 0, 0, 0))
    else:
        s_spec = pl.BlockSpec((tm // cp, GLA_H, GLA_DK, GLA_DV), lambda i: (i, 0, 0, 0))
    return pl.pallas_call(
        functools.partial(_gla_kernel, tm=tm, cp=cp, n_valid=n_valid, tiles_per_state=tiles_per_state),
        grid=(rows // tm,),
        in_specs=[x_spec, s_spec, _const_spec((1, d)), _const_spec(w["wqkvr"].shape),
                  _const_spec(w["wg1"].shape), _const_spec(w["wg2"].shape), _const_spec(w["bgk"].shape),
                  _const_spec(w["gnorm"].shape), _const_spec(w["wo"].shape), _const_spec((1, d))],
        out_specs=[x_spec, s_spec],
        out_shape=[jax.ShapeDtypeStruct(x.shape, F32), jax.ShapeDtypeStruct(s0.shape, F32)],
        scratch_shapes=[pltpu.VMEM((tm, n_proj), F32), pltpu.VMEM((tm, GLA_H * GLA_DK), F32),
                        pltpu.VMEM((tm, GLA_H * GLA_DV), F32)],
        compiler_params=_params(1),
        name="gla_layer",
    )(x, s0, w["gpre"], w["wqkvr"], w["wg1"], w["wg2"], w["bgk"], w["gnorm"], w["wo"], w["gpost"])


def _ffn_core(x, gpre_ref, wg_ref, wu_ref, cw_ref, cb_ref, wd_ref, gpost_ref, shifted, emit_u):
    fc = FFN_DIM // 2
    assert fc % LANES == 0
    h = _rms(x, gpre_ref[...]).astype(BF16)
    acc = jnp.zeros(x.shape, F32)
    for c0 in range(0, FFN_DIM, fc):
        sl = slice(c0, c0 + fc)
        u = _dot(h, wg_ref[:, sl])
        up = _dot(h, wu_ref[:, sl])
        u1, u2 = shifted(u, sl)
        c = cb_ref[:, sl] + ((cw_ref[0:1, sl] * u2 + cw_ref[1:2, sl] * u1) + cw_ref[2:3, sl] * u)
        act = (_gelu_tanh(c) * up).astype(BF16)
        acc = acc + _dot(act, wd_ref[sl, :])
        emit_u(u, sl)
    return x + _rms(acc, gpost_ref[...])


def _ffn_seq_kernel(x_ref, buf_ref, gpre_ref, wg_ref, wu_ref, cw_ref, cb_ref, wd_ref, gpost_ref,
                    y_ref, bufn_ref, carry_ref, *, tm, tiles_per_seq):
    @pl.when(pl.program_id(0) % tiles_per_seq == 0)
    def _():
        carry_ref[...] = buf_ref[0]

    row = lax.broadcasted_iota(jnp.int32, (tm, 1), 0)

    def shifted(u, sl):
        p0 = carry_ref[0:1, sl]
        p1 = carry_ref[1:2, sl]
        u1 = jnp.where(row == 0, p1, pltpu.roll(u, 1, 0))
        u2 = jnp.where(row == 0, p0, jnp.where(row == 1, p1, pltpu.roll(u, 2, 0)))
        return u1, u2

    def emit_u(u, sl):
        tail = u[tm - (CONV_W - 1):tm, :]
        carry_ref[:, sl] = tail
        bufn_ref[0, :, sl] = tail

    y_ref[...] = _ffn_core(x_ref[...], gpre_ref, wg_ref, wu_ref, cw_ref, cb_ref, wd_ref, gpost_ref,
                           shifted, emit_u)


def _ffn_rows_kernel(x_ref, prev1_ref, prev2_ref, gpre_ref, wg_ref, wu_ref, cw_ref, cb_ref, wd_ref,
                     gpost_ref, y_ref, u_ref, *, tm, tp):
    t = lax.broadcasted_iota(jnp.int32, (tm, 1), 0) & (tp - 1)

    def shifted(u, sl):
        u1 = jnp.where(t == 0, prev1_ref[:, sl], pltpu.roll(u, 1, 0))
        u2 = jnp.where(t < 2, prev2_ref[:, sl], pltpu.roll(u, 2, 0))
        return u1, u2

    def emit_u(u, sl):
        u_ref[:, sl] = u

    y_ref[...] = _ffn_core(x_ref[...], gpre_ref, wg_ref, wu_ref, cw_ref, cb_ref, wd_ref, gpost_ref,
                           shifted, emit_u)


def _ffn_weight_specs(d):
    return [_const_spec((1, d)), _const_spec((d, FFN_DIM)), _const_spec((d, FFN_DIM)),
            _const_spec((CONV_W, FFN_DIM)), _const_spec((1, FFN_DIM)), _const_spec((FFN_DIM, d)),
            _const_spec((1, d))]


def _ffn_weights(w):
    return [w[k] for k in ("gpre", "wg", "wu", "cw", "cb", "wd", "gpost")]


def _ffn_seq(x, buf, w, *, tm, tiles_per_seq):
    rows, d = x.shape
    assert rows % (tm * tiles_per_seq) == 0 and tm >= CONV_W - 1
    x_spec = pl.BlockSpec((tm, d), lambda i: (i, 0))
    b_spec = pl.BlockSpec((1, CONV_W - 1, FFN_DIM), lambda i: (i // tiles_per_seq, 0, 0))
    return pl.pallas_call(
        functools.partial(_ffn_seq_kernel, tm=tm, tiles_per_seq=tiles_per_seq),
        grid=(rows // tm,),
        in_specs=[x_spec, b_spec] + _ffn_weight_specs(d),
        out_specs=[x_spec, b_spec],
        out_shape=[jax.ShapeDtypeStruct(x.shape, F32), jax.ShapeDtypeStruct(buf.shape, F32)],
        scratch_shapes=[pltpu.VMEM((CONV_W - 1, FFN_DIM), F32)],
        compiler_params=_params(1),
        name="ffn_seq",
    )(x, buf, *_ffn_weights(w))


def _ffn_rows(x, prev1, prev2, w, *, tm, tp):
    rows, d = x.shape
    assert rows % tm == 0 and tm % tp == 0 and tp & (tp - 1) == 0
    x_spec = pl.BlockSpec((tm, d), lambda i: (i, 0))
    u_spec = pl.BlockSpec((tm, FFN_DIM), lambda i: (i, 0))
    return pl.pallas_call(
        functools.partial(_ffn_rows_kernel, tm=tm, tp=tp),
        grid=(rows // tm,),
        in_specs=[x_spec, u_spec, u_spec] + _ffn_weight_specs(d),
        out_specs=[x_spec, u_spec],
        out_shape=[jax.ShapeDtypeStruct(x.shape, F32), jax.ShapeDtypeStruct((rows, FFN_DIM), F32)],
        compiler_params=_params(1),
        name="ffn_rows",
    )(x, prev1, prev2, *_ffn_weights(w))


def _kv_kernel(x_ref, cos_ref, sin_ref, gsrc_ref, wdkv_ref, gkv_ref, wkr_ref, wkrot_ref,
               ckv_ref, kr_ref, ckvb_ref, krb_ref):
    hk = _rms(x_ref[...], gsrc_ref[...]).astype(BF16)
    ckv = _rms(_dot(hk, wdkv_ref[...]), gkv_ref[...])
    kr = _dot(hk, wkr_ref[...]) * cos_ref[...] + _dot(hk, wkrot_ref[...]) * sin_ref[...]
    ckv_ref[...] = ckv
    kr_ref[...] = kr
    ckvb_ref[...] = ckv.astype(BF16)
    krb_ref[...] = kr.astype(BF16)


def _kv_proj(x, cos, sin, w, *, tm):
    rows, d = x.shape
    period = cos.shape[0]
    assert rows % tm == 0 and period % tm == 0
    x_spec = pl.BlockSpec((tm, d), lambda i: (i, 0))
    t_spec = pl.BlockSpec((tm, MLA_ROPE), lambda i: (i % (period // tm), 0))
    c_spec = pl.BlockSpec((tm, MLA_KV_RANK), lambda i: (i, 0))
    r_spec = pl.BlockSpec((tm, MLA_ROPE), lambda i: (i, 0))
    return pl.pallas_call(
        _kv_kernel,
        grid=(rows // tm,),
        in_specs=[x_spec, t_spec, t_spec, _const_spec((1, d)), _const_spec((d, MLA_KV_RANK)),
                  _const_spec((1, MLA_KV_RANK)), _const_spec((d, MLA_ROPE)), _const_spec((d, MLA_ROPE))],
        out_specs=[c_spec, r_spec, c_spec, r_spec],
        out_shape=[jax.ShapeDtypeStruct((rows, MLA_KV_RANK), F32), jax.ShapeDtypeStruct((rows, MLA_ROPE), F32),
                   jax.ShapeDtypeStruct((rows, MLA_KV_RANK), BF16), jax.ShapeDtypeStruct((rows, MLA_ROPE), BF16)],
        compiler_params=_params(1),
        name="kv_proj",
    )(x, cos, sin, w["gsrc"], w["wdkv"], w["gkv"], w["wkr"], w["wkrot"])


def _mla_q_heads(x, cos, sin, gpre_ref, wdq_ref, qn_ref, wuq_ref, wuk_ref):
    h = _rms(x, gpre_ref[...]).astype(BF16)
    cq = _rms(_dot(h, wdq_ref[...]), qn_ref[...]).astype(BF16)
    q = _dot(cq, wuq_ref[...])
    hw = MLA_H * LANES
    for hd in range(MLA_H):
        qn = q[:, hd * MLA_NOPE:(hd + 1) * MLA_NOPE].astype(BF16)
        qa = _dot_nt(qn, wuk_ref[hd])
        roped = (q[:, hw + hd * LANES:hw + (hd + 1) * LANES] * cos
                 + q[:, 2 * hw + hd * LANES:2 * hw + (hd + 1) * LANES] * sin)
        yield hd, qa, roped[:, :MLA_ROPE]


def _lane_fold(x, op):
    parts = [x[:, c:c + LANES] for c in range(0, x.shape[1], LANES)]
    while len(parts) > 1:
        parts = [op(parts[i], parts[i + 1]) for i in range(0, len(parts) - 1, 2)] + parts[len(parts) & ~1:]
    return parts[0]


def _lane_tile(x, width):
    return x if width == LANES else jnp.concatenate([x] * (width // LANES), axis=1)


def _softmax_step(s, pv_fn, m_ref, l_ref, acc_ref, rows):
    m_prev = m_ref[rows, :]
    m_new = jnp.maximum(m_prev, jnp.max(_lane_fold(s, jnp.maximum), axis=1, keepdims=True))
    alpha = jnp.exp2(m_prev - m_new)
    p = jnp.exp2(s - _lane_tile(m_new, s.shape[1]))
    l_ref[rows, :] = alpha * l_ref[rows, :] + _lane_fold(p, jnp.add)
    acc_ref[rows, :] = _lane_tile(alpha, acc_ref.shape[1]) * acc_ref[rows, :] + pv_fn(p.astype(BF16))
    m_ref[rows, :] = m_new


def _softmax_result(l_ref, acc_ref, rows):
    return acc_ref[rows, :] / jnp.sum(l_ref[rows, :], axis=1, keepdims=True)


def _mla_out_heads(x, ol_fn, oc_ref, wuv_ref, wo_ref, gpost_ref):
    for hd in range(MLA_H):
        oc_ref[:, hd * MLA_V:(hd + 1) * MLA_V] = _dot(ol_fn(hd).astype(BF16), wuv_ref[hd])
    mix = _dot(oc_ref[...].astype(BF16), wo_ref[...])
    return x + _rms(mix, gpost_ref[...])


def _mla_prompt_kernel(x_ref, ckv_ref, kr_ref, cos_ref, sin_ref, gpre_ref, wdq_ref, qn_ref, wuq_ref,
                       wuk_ref, wuv_ref, wo_ref, gpost_ref, y_ref,
                       qa_ref, qr_ref, m_ref, l_ref, acc_ref, oc_ref, *, tq):
    i = pl.program_id(1)
    x = x_ref[...]
    for hd, qa, qr in _mla_q_heads(x, cos_ref[...], sin_ref[...], gpre_ref, wdq_ref, qn_ref, wuq_ref, wuk_ref):
        qa_ref[hd * tq:(hd + 1) * tq, :] = qa.astype(BF16)
        qr_ref[hd * tq:(hd + 1) * tq, :] = qr.astype(BF16)
    m_ref[...] = jnp.full(m_ref.shape, -jnp.inf, F32)
    l_ref[...] = jnp.zeros(l_ref.shape, F32)
    acc_ref[...] = jnp.zeros(acc_ref.shape, F32)

    gm = MLA_HEAD_GROUP * tq

    def key_block(j, causal):
        k0 = pl.multiple_of(j * tq, tq)
        kc = ckv_ref[pl.ds(k0, tq), :]
        kk = kr_ref[pl.ds(k0, tq), :]
        if causal:
            visible = (lax.broadcasted_iota(jnp.int32, (gm, tq), 1)
                       <= (lax.broadcasted_iota(jnp.int32, (gm, tq), 0) & (tq - 1)))
        for g in range(MLA_H // MLA_HEAD_GROUP):
            rows = slice(g * gm, (g + 1) * gm)
            s = (_dot_nt(qa_ref[rows, :], kc) + _dot_nt(qr_ref[rows, :], kk)) * SM_SCALE_LOG2
            if causal:
                s = jnp.where(visible, s, -jnp.inf)
            _softmax_step(s, lambda p: _dot(p, kc), m_ref, l_ref, acc_ref, rows)

    def body(j, carry):
        key_block(j, False)
        return carry

    lax.fori_loop(0, i, body, 0)
    key_block(i, True)

    def ol_fn(hd):
        return _softmax_result(l_ref, acc_ref, slice(hd * tq, (hd + 1) * tq))

    y_ref[...] = _mla_out_heads(x, ol_fn, oc_ref, wuv_ref, wo_ref, gpost_ref)


def _mla_weight_specs(w, keys):
    return [_const_spec(w[k].shape) for k in keys]


MLA_HEAD_GROUP = 8
MLA_Q_KEYS = ("gpre", "wdq", "qnorm", "wuq", "wuk")
MLA_O_KEYS = ("wuv", "wo", "gpost")


def _mla_prompt(x, ckv_b, kr_b, cos, sin, w, *, tq, seq):
    rows, d = x.shape
    nt = seq // tq
    assert rows % seq == 0 and seq % tq == 0
    m = MLA_H * tq
    x_spec = pl.BlockSpec((tq, d), lambda b, i: (b * nt + i, 0))
    t_spec = pl.BlockSpec((tq, LANES), lambda b, i: (i, 0))
    keys = MLA_Q_KEYS + MLA_O_KEYS
    return pl.pallas_call(
        functools.partial(_mla_prompt_kernel, tq=tq),
        grid=(rows // seq, nt),
        in_specs=[x_spec,
                  pl.BlockSpec((seq, MLA_KV_RANK), lambda b, i: (b, 0)),
                  pl.BlockSpec((seq, MLA_ROPE), lambda b, i: (b, 0)),
                  t_spec, t_spec] + _mla_weight_specs(w, keys),
        out_specs=x_spec,
        out_shape=jax.ShapeDtypeStruct(x.shape, F32),
        scratch_shapes=[pltpu.VMEM((m, MLA_KV_RANK), BF16), pltpu.VMEM((m, MLA_ROPE), BF16),
                        pltpu.VMEM((m, LANES), F32), pltpu.VMEM((m, LANES), F32), pltpu.VMEM((m, MLA_KV_RANK), F32),
                        pltpu.VMEM((tq, MLA_H * MLA_V), F32)],
        compiler_params=_params(2),
        name="mla_prompt",
    )(x, ckv_b, kr_b, cos, sin, *[w[k] for k in keys])


def _mla_q_kernel(x_ref, cos_ref, sin_ref, gpre_ref, wdq_ref, qn_ref, wuq_ref, wuk_ref, qa_ref, qr_ref):
    for hd, qa, qr in _mla_q_heads(x_ref[...], cos_ref[...], sin_ref[...], gpre_ref, wdq_ref, qn_ref,
                                   wuq_ref, wuk_ref):
        qa_ref[hd] = qa
        qr_ref[hd] = qr


def _mla_q(x, cos, sin, w, *, tm):
    rows, d = x.shape
    assert rows % tm == 0
    return pl.pallas_call(
        _mla_q_kernel,
        grid=(rows // tm,),
        in_specs=[pl.BlockSpec((tm, d), lambda i: (i, 0)), pl.BlockSpec((tm, LANES), lambda i: (i, 0)),
                  pl.BlockSpec((tm, LANES), lambda i: (i, 0))] + _mla_weight_specs(w, MLA_Q_KEYS),
        out_specs=[pl.BlockSpec((MLA_H, tm, MLA_KV_RANK), lambda i: (0, i, 0)),
                   pl.BlockSpec((MLA_H, tm, MLA_ROPE), lambda i: (0, i, 0))],
        out_shape=[jax.ShapeDtypeStruct((MLA_H, rows, MLA_KV_RANK), F32),
                   jax.ShapeDtypeStruct((MLA_H, rows, MLA_ROPE), F32)],
        compiler_params=_params(1),
        name="mla_q",
    )(x, cos, sin, *[w[k] for k in MLA_Q_KEYS])


def _mla_attend_sample_kernel(pt_ref, qa_ref, qr_ref, ckvn_ref, krn_ref, *rest, tq, n_valid, pages):
    del pt_ref
    ckv_pages = rest[:pages]
    krt_pages = rest[pages:2 * pages]
    ol_ref, qa_s, qr_s, m_ref, l_ref, acc_ref = rest[2 * pages:]
    j = pl.program_id(1)
    every = slice(None)

    @pl.when(j == 0)
    def _():
        for hd in range(MLA_H):
            qa_s[hd * tq:(hd + 1) * tq, :] = qa_ref[hd].astype(BF16)
            qr_s[hd * tq:(hd + 1) * tq, :] = qr_ref[hd].astype(BF16)
        m_ref[...] = jnp.full(m_ref.shape, -jnp.inf, F32)
        l_ref[...] = jnp.zeros(l_ref.shape, F32)
        acc_ref[...] = jnp.zeros(acc_ref.shape, F32)

    qa = qa_s[...]
    qr = qr_s[...]
    kcs = [ckv_pages[i][0].astype(BF16) for i in range(pages)]
    s = jnp.concatenate([_dot_nt(qa, kcs[i]) + _dot(qr, krt_pages[i][0].astype(BF16)) for i in range(pages)],
                        axis=1) * SM_SCALE_LOG2
    page = kcs[0].shape[0]

    def pv_pages(p):
        out = _dot(p[:, :page], kcs[0])
        for i in range(1, pages):
            out = out + _dot(p[:, i * page:(i + 1) * page], kcs[i])
        return out

    _softmax_step(s, pv_pages, m_ref, l_ref, acc_ref, every)

    @pl.when(j == pl.num_programs(1) - 1)
    def _():
        kc = ckvn_ref[0].astype(BF16)
        kk = krn_ref[0].astype(BF16)
        n_new = kc.shape[0]
        qpos = lax.broadcasted_iota(jnp.int32, (MLA_H * tq, 1), 0) & (tq - 1)
        kpos = lax.broadcasted_iota(jnp.int32, (1, n_new), 1)
        visible = (kpos <= qpos) & (kpos < n_valid)
        sn = jnp.where(visible, (_dot_nt(qa, kc) + _dot_nt(qr, kk)) * SM_SCALE_LOG2, -jnp.inf)
        _softmax_step(sn, lambda p: _dot(p, kc), m_ref, l_ref, acc_ref, every)
        for hd in range(MLA_H):
            ol_ref[hd] = _softmax_result(l_ref, acc_ref, slice(hd * tq, (hd + 1) * tq))


def _mla_attend_sample(qa, qr, ckv_new, kr_new, cache_ckv, cache_krt, page_table, *, tq, n_valid, pages):
    bsz, n_pages = page_table.shape
    page = cache_ckv.shape[1]
    assert n_pages % pages == 0 and tq & (tq - 1) == 0 and ckv_new.shape[1] == page
    m = MLA_H * tq
    q_spec = pl.BlockSpec((MLA_H, tq, MLA_KV_RANK), lambda b, j, pt: (0, b, 0))
    r_spec = pl.BlockSpec((MLA_H, tq, MLA_ROPE), lambda b, j, pt: (0, b, 0))
    page_specs = (
        [pl.BlockSpec((1, page, MLA_KV_RANK), lambda b, j, pt, i=i: (pt[b, j * pages + i], 0, 0))
         for i in range(pages)]
        + [pl.BlockSpec((1, MLA_ROPE, page), lambda b, j, pt, i=i: (pt[b, j * pages + i], 0, 0))
           for i in range(pages)])
    grid_spec = pltpu.PrefetchScalarGridSpec(
        num_scalar_prefetch=1,
        grid=(bsz, n_pages // pages),
        in_specs=[q_spec, r_spec,
                  pl.BlockSpec((1, page, MLA_KV_RANK), lambda b, j, pt: (b, 0, 0)),
                  pl.BlockSpec((1, page, MLA_ROPE), lambda b, j, pt: (b, 0, 0))] + page_specs,
        out_specs=q_spec,
        scratch_shapes=[pltpu.VMEM((m, MLA_KV_RANK), BF16), pltpu.VMEM((m, MLA_ROPE), BF16),
                        pltpu.VMEM((m, LANES), F32), pltpu.VMEM((m, LANES), F32), pltpu.VMEM((m, MLA_KV_RANK), F32)],
    )
    return pl.pallas_call(
        functools.partial(_mla_attend_sample_kernel, tq=tq, n_valid=n_valid, pages=pages),
        grid_spec=grid_spec,
        out_shape=jax.ShapeDtypeStruct(qa.shape, F32),
        compiler_params=_params(2),
        name="mla_attend_sample",
    )(page_table, qa, qr, ckv_new, kr_new, *([cache_ckv] * pages), *([cache_krt] * pages))


def _mla_out_kernel(x_ref, ol_ref, wuv_ref, wo_ref, gpost_ref, y_ref, oc_ref):
    y_ref[...] = _mla_out_heads(x_ref[...], lambda hd: ol_ref[hd], oc_ref, wuv_ref, wo_ref, gpost_ref)


def _mla_out(x, ol, w, *, tm):
    rows, d = x.shape
    assert rows % tm == 0
    x_spec = pl.BlockSpec((tm, d), lambda i: (i, 0))
    return pl.pallas_call(
        _mla_out_kernel,
        grid=(rows // tm,),
        in_specs=[x_spec, pl.BlockSpec((MLA_H, tm, MLA_KV_RANK), lambda i: (0, i, 0))]
        + _mla_weight_specs(w, MLA_O_KEYS),
        out_specs=x_spec,
        out_shape=jax.ShapeDtypeStruct(x.shape, F32),
        scratch_shapes=[pltpu.VMEM((tm, MLA_H * MLA_V), F32)],
        compiler_params=_params(1),
        name="mla_out",
    )(x, ol, *[w[k] for k in MLA_O_KEYS])


def _row(v):
    return v.reshape(1, -1).astype(F32)


def _rot_cols(w):
    half = w.shape[-1] // 2
    return jnp.concatenate([-w[..., half:], w[..., :half]], axis=-1)


def _prep_weights(p):
    gla, ffn, mla = [], [], []
    for l in range(N_GLA):
        rank = p["gla_w_gk1"].shape[-1]
        gla.append(dict(
            gpre=_row(p["norm_mix_pre"][l]), gpost=_row(p["norm_mix_post"][l]),
            wqkvr=jnp.concatenate([p["gla_w_q"][l], p["gla_w_k"][l], p["gla_w_v"][l], p["gla_w_r"][l]],
                                  axis=1).astype(BF16),
            wg1=jnp.pad(p["gla_w_gk1"][l], ((0, 0), (0, LANES - rank))).astype(BF16),
            wg2=jnp.pad(p["gla_w_gk2"][l], ((0, LANES - rank), (0, 0))).astype(BF16),
            bgk=_row(p["gla_b_gk"][l]), gnorm=_row(p["gla_norm"][l]), wo=p["gla_w_o"][l].astype(BF16)))
    for l in range(DEPTH):
        ffn.append(dict(
            gpre=_row(p["norm_ffn_pre"][l]), gpost=_row(p["norm_ffn_post"][l]),
            wg=p["ffn_w_gate"][l].astype(BF16), wu=p["ffn_w_up"][l].astype(BF16),
            cw=p["ffn_conv_w"][l].astype(F32), cb=_row(p["ffn_conv_b"][l]), wd=p["ffn_w_down"][l].astype(BF16)))
    wuk = jnp.transpose(p["mla_w_uk"], (1, 0, 2)).astype(BF16)
    wuv = jnp.transpose(p["mla_w_uv"], (1, 0, 2)).astype(BF16)
    for j in range(DEPTH - N_GLA):
        l = N_GLA + j
        wuq = p["mla_w_uq"][j].reshape(MLA_Q_RANK, MLA_H, MLA_NOPE + MLA_ROPE)
        nope = wuq[:, :, :MLA_NOPE].reshape(MLA_Q_RANK, MLA_H * MLA_NOPE)
        rope = wuq[:, :, MLA_NOPE:]
        lane_pad = ((0, 0), (0, 0), (0, LANES - MLA_ROPE))
        rope_p = jnp.pad(rope, lane_pad).reshape(MLA_Q_RANK, MLA_H * LANES)
        rot_p = jnp.pad(_rot_cols(rope), lane_pad).reshape(MLA_Q_RANK, MLA_H * LANES)
        mla.append(dict(
            gpre=_row(p["norm_mix_pre"][l]), gpost=_row(p["norm_mix_post"][l]),
            wdq=p["mla_w_dq"][j].astype(BF16), qnorm=_row(p["mla_q_norm"][j]),
            wuq=jnp.concatenate([nope, rope_p, rot_p], axis=1).astype(BF16),
            wuk=wuk, wuv=wuv, wo=p["mla_w_o"][j].astype(BF16)))
    kv = dict(gsrc=_row(p["mla_kv_src_norm"]), wdkv=p["mla_w_dkv"].astype(BF16), gkv=_row(p["mla_kv_norm"]),
              wkr=p["mla_w_kr"].astype(BF16), wkrot=_rot_cols(p["mla_w_kr"]).astype(BF16))
    return gla, ffn, mla, kv


def _rope_tables(pos):
    half = MLA_ROPE // 2
    inv = ROPE_THETA ** (-jnp.arange(half, dtype=F32) / half)
    ang = pos.astype(F32)[:, None] * inv[None, :]
    cos, sin = jnp.cos(ang), jnp.sin(ang)
    return jnp.concatenate([cos, cos], axis=-1), jnp.concatenate([sin, sin], axis=-1)


def _trunk_prompt(x, weights):
    gla_w, ffn_w, mla_w, kv_w = weights
    bsz, t, d = x.shape
    x = x.reshape(bsz * t, d)
    cos, sin = _rope_tables(jnp.arange(t, dtype=jnp.int32))
    cos2, sin2 = jnp.concatenate([cos, cos], -1), jnp.concatenate([sin, sin], -1)
    s0 = jnp.zeros((bsz, GLA_H, GLA_DK, GLA_DV), F32)
    buf0 = jnp.zeros((bsz, CONV_W - 1, FFN_DIM), F32)
    tm_gla, tm_ffn, tm_kv, tq = 256, 512, 512, 256
    states, bufs = [], []
    ckv = kr = ckv_b = kr_b = None
    for l in range(DEPTH):
        if l < N_GLA:
            x, s = _gla_layer(x, s0, gla_w[l], tm=tm_gla, cp=GLA_CHUNK, n_valid=GLA_CHUNK,
                              tiles_per_state=t // tm_gla)
            states.append(s)
        else:
            if l == N_GLA:
                ckv, kr, ckv_b, kr_b = _kv_proj(x, cos, sin, kv_w, tm=tm_kv)
            x = _mla_prompt(x, ckv_b, kr_b, cos2, sin2, mla_w[l - N_GLA], tq=tq, seq=t)
        x, b = _ffn_seq(x, buf0, ffn_w[l], tm=tm_ffn, tiles_per_seq=t // tm_ffn)
        bufs.append(b)
    return (x.reshape(bsz, t, d), jnp.stack(states), jnp.stack(bufs),
            ckv.reshape(bsz, t, -1), kr.reshape(bsz, t, -1))


def _trunk_sample(x, state_gla, state_conv, cache_ckv, cache_kr, page_table, weights):
    gla_w, ffn_w, mla_w, kv_w = weights
    bsz, t, d = x.shape
    tp = -(-t // SUBLANES) * SUBLANES
    assert tp & (tp - 1) == 0 and t >= CONV_W - 1
    rows = bsz * tp
    page = cache_ckv.shape[1]
    past = page_table.shape[1] * page
    x = jnp.pad(x, ((0, 0), (0, tp - t), (0, 0))).reshape(rows, d)
    cos, sin = _rope_tables(past + jnp.arange(tp, dtype=jnp.int32))
    cos, sin = jnp.tile(cos, (bsz, 1)), jnp.tile(sin, (bsz, 1))
    cos2, sin2 = jnp.concatenate([cos, cos], -1), jnp.concatenate([sin, sin], -1)
    cache_krt = jnp.swapaxes(cache_kr, 1, 2)
    tm_gla = 8 * tp
    pad_tail = ((0, 0), (0, tp - (CONV_W - 1)), (0, 0))
    states, bufs = [], []
    ckv = kr = ckv_pad = kr_pad = None
    for l in range(DEPTH):
        if l < N_GLA:
            x, s = _gla_layer(x, state_gla[l], gla_w[l], tm=tm_gla, cp=tp, n_valid=t, tiles_per_state=0)
            states.append(s)
        else:
            w = mla_w[l - N_GLA]
            if l == N_GLA:
                ckv, kr, _, _ = _kv_proj(x, cos, sin, kv_w, tm=rows)
                ckv = ckv.reshape(bsz, tp, -1)
                kr = kr.reshape(bsz, tp, -1)
                ckv_pad = jnp.pad(ckv, ((0, 0), (0, page - tp), (0, 0)))
                kr_pad = jnp.pad(kr, ((0, 0), (0, page - tp), (0, 0)))
            qa, qr = _mla_q(x, cos2, sin2, w, tm=rows)
            ol = _mla_attend_sample(qa, qr, ckv_pad, kr_pad, cache_ckv, cache_krt, page_table,
                                    tq=tp, n_valid=t, pages=16)
            x = _mla_out(x, ol, w, tm=rows)
        buf = state_conv[l]
        prev1 = jnp.pad(buf[:, CONV_W - 2:], ((0, 0), (0, tp - 1), (0, 0))).reshape(rows, FFN_DIM)
        prev2 = jnp.pad(buf, pad_tail).reshape(rows, FFN_DIM)
        x, u = _ffn_rows(x, prev1, prev2, ffn_w[l], tm=rows, tp=tp)
        bufs.append(u.reshape(bsz, tp, FFN_DIM)[:, t - (CONV_W - 1):t])
    return (x.reshape(bsz, tp, d)[:, :t], jnp.stack(states), jnp.stack(bufs), ckv[:, :t], kr[:, :t])


def kernel(x_prompt, x_sample, state_gla, state_ffn_conv, cache_ckv, cache_krope, page_table, norm_mix_pre, norm_mix_post, norm_ffn_pre, norm_ffn_post, ffn_w_gate, ffn_w_up, ffn_conv_w, ffn_conv_b, ffn_w_down, gla_w_q, gla_w_k, gla_w_v, gla_w_gk1, gla_w_gk2, gla_b_gk, gla_w_r, gla_norm, gla_w_o, mla_kv_src_norm, mla_w_dkv, mla_kv_norm, mla_w_kr, mla_w_uk, mla_w_uv, mla_w_dq, mla_q_norm, mla_w_uq, mla_w_o):
    p = dict(norm_mix_pre=norm_mix_pre, norm_mix_post=norm_mix_post, norm_ffn_pre=norm_ffn_pre,
             norm_ffn_post=norm_ffn_post, ffn_w_gate=ffn_w_gate, ffn_w_up=ffn_w_up, ffn_conv_w=ffn_conv_w,
             ffn_conv_b=ffn_conv_b, ffn_w_down=ffn_w_down, gla_w_q=gla_w_q, gla_w_k=gla_w_k, gla_w_v=gla_w_v,
             gla_w_gk1=gla_w_gk1, gla_w_gk2=gla_w_gk2, gla_b_gk=gla_b_gk, gla_w_r=gla_w_r, gla_norm=gla_norm,
             gla_w_o=gla_w_o, mla_kv_src_norm=mla_kv_src_norm, mla_w_dkv=mla_w_dkv, mla_kv_norm=mla_kv_norm,
             mla_w_kr=mla_w_kr, mla_w_uk=mla_w_uk, mla_w_uv=mla_w_uv, mla_w_dq=mla_w_dq, mla_q_norm=mla_q_norm,
             mla_w_uq=mla_w_uq, mla_w_o=mla_w_o)
    weights = _prep_weights(p)
    y_p, gla_p, conv_p, ckv_p, kr_p = _trunk_prompt(x_prompt, weights)
    y_s, gla_s, conv_s, ckv_s, kr_s = _trunk_sample(x_sample, state_gla, state_ffn_conv, cache_ckv,
                                                    cache_krope, page_table, weights)
    return (y_p, y_s, gla_p, gla_s, conv_p, conv_s, ckv_p, ckv_s, kr_p, kr_s)
```

```python
import functools

import jax
import jax.numpy as jnp
from jax import lax
from jax.experimental import pallas as pl
from jax.experimental.pallas import tpu as pltpu

D_MODEL = 1024
DEPTH = 4
N_GLA = DEPTH // 2
GLA_H = 4
GLA_DK = 128
GLA_DV = 256
GLA_GATE_TEMP = 16.0
GLA_CHUNK = 64
MLA_H = 8
MLA_NOPE = 128
MLA_ROPE = 64
MLA_V = 128
MLA_Q_RANK = 384
MLA_KV_RANK = 256
ROPE_THETA = 10000.0
FFN_DIM = 2816
CONV_W = 3
EPS = 1e-6
LANES = 128
SUBLANES = 8
VMEM_LIMIT = 56 * 1024 * 1024

F32 = jnp.float32
BF16 = jnp.bfloat16
NT_DIMS = (((1,), (1,)), ((), ()))
TN_DIMS = (((0,), (0,)), ((), ()))
LOG2_E = 1.4426950408889634
SM_SCALE_LOG2 = (MLA_NOPE + MLA_ROPE) ** -0.5 * LOG2_E


def _rms(x, g):
    return x * lax.rsqrt(jnp.mean(x * x, axis=-1, keepdims=True) + EPS) * g


def _dot(a, b):
    return jnp.dot(a, b, preferred_element_type=F32)


def _dot_nt(a, b):
    return lax.dot_general(a, b, NT_DIMS, preferred_element_type=F32)


def _log_sigmoid(z):
    return jnp.minimum(z, 0.0) - jnp.log1p(jnp.exp(-jnp.abs(z)))


def _gelu_tanh(x):
    cdf = 0.5 * (1.0 + jnp.tanh(0.7978845608028654 * (x + 0.044715 * (x * x * x))))
    return x * cdf


def _const_spec(shape):
    nd = len(shape)
    return pl.BlockSpec(shape, lambda *_: (0,) * nd, pipeline_mode=pl.Buffered(1))


def _params(n_axes):
    return pltpu.CompilerParams(dimension_semantics=("arbitrary",) * n_axes,
                                vmem_limit_bytes=VMEM_LIMIT)


def _gla_kernel(x_ref, s0_ref, gpre_ref, wqkvr_ref, wg1_ref, wg2_ref, bgk_ref, gn_ref, wo_ref,
                gpost_ref, y_ref, s_ref, proj_ref, lg_ref, o_ref, *, tm, cp, n_valid, tiles_per_state):
    if tiles_per_state:
        @pl.when(pl.program_id(0) % tiles_per_state == 0)
        def _():
            s_ref[...] = s0_ref[...]
    else:
        s_ref[...] = s0_ref[...]

    x = x_ref[...]
    h = _rms(x, gpre_ref[...]).astype(BF16)
    proj_ref[...] = _dot(h, wqkvr_ref[...])
    g1 = _dot(h, wg1_ref[...])
    z = _dot(g1.astype(BF16), wg2_ref[...]) + bgk_ref[...]
    lg = _log_sigmoid(z) / GLA_GATE_TEMP
    if n_valid < cp:
        t_in_chunk = lax.broadcasted_iota(jnp.int32, (tm, 1), 0) & (cp - 1)
        lg = jnp.where(t_in_chunk < n_valid, lg, 0.0)
    lg_ref[...] = lg

    nc = tm // cp
    r_i = lax.broadcasted_iota(jnp.int32, (tm, tm), 0)
    c_i = lax.broadcasted_iota(jnp.int32, (tm, tm), 1)
    intra = (r_i >= c_i) & ((r_i & -cp) == (c_i & -cp))
    t_in = lax.broadcasted_iota(jnp.int32, (tm, GLA_DK), 0) & (cp - 1)
    eye = (lax.broadcasted_iota(jnp.int32, (GLA_DK, GLA_DK), 0)
           == lax.broadcasted_iota(jnp.int32, (GLA_DK, GLA_DK), 1))
    k_off = GLA_H * GLA_DK
    v_off = 2 * GLA_H * GLA_DK
    r_off = v_off + GLA_H * GLA_DV

    for hd in range(GLA_H):
        kcol = slice(hd * GLA_DK, (hd + 1) * GLA_DK)
        vcol = slice(hd * GLA_DV, (hd + 1) * GLA_DV)
        b = lg_ref[:, kcol]
        shift = 1
        while shift < cp:
            b = b + jnp.where(t_in >= shift, pltpu.roll(b, shift, 0), 0.0)
            shift *= 2
        bl = jnp.concatenate(
            [jnp.broadcast_to(b[(c + 1) * cp - 1:(c + 1) * cp, :], (cp, GLA_DK)) for c in range(nc)], axis=0)
        q = proj_ref[:, hd * GLA_DK:(hd + 1) * GLA_DK] * (GLA_DK ** -0.5)
        k = proj_ref[:, k_off + hd * GLA_DK:k_off + (hd + 1) * GLA_DK]
        v = proj_ref[:, v_off + hd * GLA_DV:v_off + (hd + 1) * GLA_DV]
        qe = q * jnp.exp(b)
        ke = k * jnp.exp(-b)
        kd = k * jnp.exp(bl - b)
        a = jnp.where(intra, _dot_nt(qe.astype(BF16), ke.astype(BF16)), 0.0).astype(BF16)
        o_intra = _dot(a, v.astype(BF16))
        s = s_ref[0, hd]
        for c in range(nc):
            rows = slice(c * cp, (c + 1) * cp)
            if not tiles_per_state:
                s = s_ref[c, hd]
            o_ref[rows, vcol] = o_intra[rows, :] + _dot(qe[rows, :].astype(BF16), s.astype(BF16))
            dcol = jnp.sum(jnp.where(eye, jnp.exp(bl[c * cp:c * cp + 1, :]), 0.0), axis=1, keepdims=True)
            s = dcol * s + lax.dot_general(kd[rows, :].astype(BF16), v[rows, :].astype(BF16), TN_DIMS,
                                           preferred_element_type=F32)
            if not tiles_per_state:
                s_ref[c, hd] = s
        if tiles_per_state:
            s_ref[0, hd] = s

    gn = gn_ref[...]
    for hd in range(GLA_H):
        vcol = slice(hd * GLA_DV, (hd + 1) * GLA_DV)
        r = proj_ref[:, r_off + hd * GLA_DV:r_off + (hd + 1) * GLA_DV]
        o_ref[:, vcol] = _rms(o_ref[:, vcol], gn) * (r * jax.nn.sigmoid(r))
    mix = _dot(o_ref[...].astype(BF16), wo_ref[...])
    y_ref[...] = x + _rms(mix, gpost_ref[...])


def _gla_layer(x, s0_all, layer, w, *, tm, cp, n_valid, tiles_per_state):
    rows, d = x.shape
    assert rows % tm == 0 and tm % cp == 0 and cp & (cp - 1) == 0
    n_proj = w["wqkvr"].shape[1]
    x_spec = pl.BlockSpec((tm, d), lambda i: (i, 0))
    n_blk = 1 if tiles_per_state else tm // cp
    per = tiles_per_state or 1
    s_in_spec = pl.BlockSpec((None, n_blk, GLA_H, GLA_DK, GLA_DV), lambda i: (layer, i // per, 0, 0, 0))
    s_spec = pl.BlockSpec((n_blk, GLA_H, GLA_DK, GLA_DV), lambda i: (i // per, 0, 0, 0))
    return pl.pallas_call(
        functools.partial(_gla_kernel, tm=tm, cp=cp, n_valid=n_valid, tiles_per_state=tiles_per_state),
        grid=(rows // tm,),
        in_specs=[x_spec, s_in_spec, _const_spec((1, d)), _const_spec(w["wqkvr"].shape),
                  _const_spec(w["wg1"].shape), _const_spec(w["wg2"].shape), _const_spec(w["bgk"].shape),
                  _const_spec(w["gnorm"].shape), _const_spec(w["wo"].shape), _const_spec((1, d))],
        out_specs=[x_spec, s_spec],
        out_shape=[jax.ShapeDtypeStruct(x.shape, F32), jax.ShapeDtypeStruct(s0_all.shape[1:], F32)],
        scratch_shapes=[pltpu.VMEM((tm, n_proj), F32), pltpu.VMEM((tm, GLA_H * GLA_DK), F32),
                        pltpu.VMEM((tm, GLA_H * GLA_DV), F32)],
        compiler_params=_params(1),
        name="gla_layer",
    )(x, s0_all, w["gpre"], w["wqkvr"], w["wg1"], w["wg2"], w["bgk"], w["gnorm"], w["wo"], w["gpost"])


def _ffn_core(x, gpre_ref, wg_ref, wu_ref, cw_ref, cb_ref, wd_ref, gpost_ref, shifted, emit_u):
    fc = FFN_DIM // 2
    assert fc % LANES == 0
    h = _rms(x, gpre_ref[...]).astype(BF16)
    acc = jnp.zeros(x.shape, F32)
    for c0 in range(0, FFN_DIM, fc):
        sl = slice(c0, c0 + fc)
        u = _dot(h, wg_ref[:, sl])
        up = _dot(h, wu_ref[:, sl])
        u1, u2 = shifted(u, sl)
        c = cb_ref[:, sl] + ((cw_ref[0:1, sl] * u2 + cw_ref[1:2, sl] * u1) + cw_ref[2:3, sl] * u)
        act = (_gelu_tanh(c) * up).astype(BF16)
        acc = acc + _dot(act, wd_ref[sl, :])
        emit_u(u, sl)
    return x + _rms(acc, gpost_ref[...])


def _ffn_seq_kernel(x_ref, buf_ref, gpre_ref, wg_ref, wu_ref, cw_ref, cb_ref, wd_ref, gpost_ref,
                    y_ref, bufn_ref, carry_ref, *, tm, tiles_per_seq):
    @pl.when(pl.program_id(0) % tiles_per_seq == 0)
    def _():
        carry_ref[...] = buf_ref[0]

    row = lax.broadcasted_iota(jnp.int32, (tm, 1), 0)

    def shifted(u, sl):
        p0 = carry_ref[0:1, sl]
        p1 = carry_ref[1:2, sl]
        u1 = jnp.where(row == 0, p1, pltpu.roll(u, 1, 0))
        u2 = jnp.where(row == 0, p0, jnp.where(row == 1, p1, pltpu.roll(u, 2, 0)))
        return u1, u2

    def emit_u(u, sl):
        tail = u[tm - (CONV_W - 1):tm, :]
        carry_ref[:, sl] = tail
        bufn_ref[0, :, sl] = tail

    y_ref[...] = _ffn_core(x_ref[...], gpre_ref, wg_ref, wu_ref, cw_ref, cb_ref, wd_ref, gpost_ref,
                           shifted, emit_u)


def _ffn_rows_kernel(x_ref, prev1_ref, prev2_ref, gpre_ref, wg_ref, wu_ref, cw_ref, cb_ref, wd_ref,
                     gpost_ref, y_ref, u_ref, *, tm, tp):
    t = lax.broadcasted_iota(jnp.int32, (tm, 1), 0) & (tp - 1)

    def shifted(u, sl):
        u1 = jnp.where(t == 0, prev1_ref[:, sl], pltpu.roll(u, 1, 0))
        u2 = jnp.where(t < 2, prev2_ref[:, sl], pltpu.roll(u, 2, 0))
        return u1, u2

    def emit_u(u, sl):
        u_ref[:, sl] = u

    y_ref[...] = _ffn_core(x_ref[...], gpre_ref, wg_ref, wu_ref, cw_ref, cb_ref, wd_ref, gpost_ref,
                           shifted, emit_u)


FFN_KEYS = ("gpre", "wg", "wu", "cw", "cb", "wd", "gpost")


def _ffn_weight_specs(w, layer):
    return [pl.BlockSpec((None,) + w[k].shape[1:], lambda *_: (layer, 0, 0), pipeline_mode=pl.Buffered(1))
            for k in FFN_KEYS]


def _ffn_weights(w):
    return [w[k] for k in FFN_KEYS]


def _ffn_seq(x, buf, w, layer, *, tm, tiles_per_seq):
    rows, d = x.shape
    assert rows % (tm * tiles_per_seq) == 0 and tm >= CONV_W - 1
    x_spec = pl.BlockSpec((tm, d), lambda i: (i, 0))
    b_spec = pl.BlockSpec((1, CONV_W - 1, FFN_DIM), lambda i: (i // tiles_per_seq, 0, 0))
    return pl.pallas_call(
        functools.partial(_ffn_seq_kernel, tm=tm, tiles_per_seq=tiles_per_seq),
        grid=(rows // tm,),
        in_specs=[x_spec, b_spec] + _ffn_weight_specs(w, layer),
        out_specs=[x_spec, b_spec],
        out_shape=[jax.ShapeDtypeStruct(x.shape, F32), jax.ShapeDtypeStruct(buf.shape, F32)],
        scratch_shapes=[pltpu.VMEM((CONV_W - 1, FFN_DIM), F32)],
        compiler_params=_params(1),
        name="ffn_seq",
    )(x, buf, *_ffn_weights(w))


def _ffn_rows(x, prev1, prev2, w, layer, *, tm, tp):
    rows, d = x.shape
    assert rows % tm == 0 and tm % tp == 0 and tp & (tp - 1) == 0
    x_spec = pl.BlockSpec((tm, d), lambda i: (i, 0))
    u_spec = pl.BlockSpec((tm, FFN_DIM), lambda i: (i, 0))
    return pl.pallas_call(
        functools.partial(_ffn_rows_kernel, tm=tm, tp=tp),
        grid=(rows // tm,),
        in_specs=[x_spec, u_spec, u_spec] + _ffn_weight_specs(w, layer),
        out_specs=[x_spec, u_spec],
        out_shape=[jax.ShapeDtypeStruct(x.shape, F32), jax.ShapeDtypeStruct((rows, FFN_DIM), F32)],
        compiler_params=_params(1),
        name="ffn_rows",
    )(x, prev1, prev2, *_ffn_weights(w))


def _kv_kernel(x_ref, cos_ref, sin_ref, gsrc_ref, wdkv_ref, gkv_ref, wkr_ref, wkrot_ref,
               ckv_ref, kr_ref, ckvb_ref, krb_ref):
    hk = _rms(x_ref[...], gsrc_ref[...]).astype(BF16)
    ckv = _rms(_dot(hk, wdkv_ref[...]), gkv_ref[...])
    kr = _dot(hk, wkr_ref[...]) * cos_ref[...] + _dot(hk, wkrot_ref[...]) * sin_ref[...]
    ckv_ref[...] = ckv
    kr_ref[...] = kr
    ckvb_ref[...] = ckv.astype(BF16)
    krb_ref[...] = kr.astype(BF16)


def _kv_proj(x, cos, sin, w, *, tm):
    rows, d = x.shape
    period = cos.shape[0]
    assert rows % tm == 0 and period % tm == 0
    x_spec = pl.BlockSpec((tm, d), lambda i: (i, 0))
    t_spec = pl.BlockSpec((tm, MLA_ROPE), lambda i: (i % (period // tm), 0))
    c_spec = pl.BlockSpec((tm, MLA_KV_RANK), lambda i: (i, 0))
    r_spec = pl.BlockSpec((tm, MLA_ROPE), lambda i: (i, 0))
    return pl.pallas_call(
        _kv_kernel,
        grid=(rows // tm,),
        in_specs=[x_spec, t_spec, t_spec, _const_spec((1, d)), _const_spec((d, MLA_KV_RANK)),
                  _const_spec((1, MLA_KV_RANK)), _const_spec((d, MLA_ROPE)), _const_spec((d, MLA_ROPE))],
        out_specs=[c_spec, r_spec, c_spec, r_spec],
        out_shape=[jax.ShapeDtypeStruct((rows, MLA_KV_RANK), F32), jax.ShapeDtypeStruct((rows, MLA_ROPE), F32),
                   jax.ShapeDtypeStruct((rows, MLA_KV_RANK), BF16), jax.ShapeDtypeStruct((rows, MLA_ROPE), BF16)],
        compiler_params=_params(1),
        name="kv_proj",
    )(x, cos, sin, w["gsrc"], w["wdkv"], w["gkv"], w["wkr"], w["wkrot"])


def _mla_q_heads(x, cos, sin, gpre_ref, wdq_ref, qn_ref, wuq_ref, wuk_ref):
    h = _rms(x, gpre_ref[...]).astype(BF16)
    cq = _rms(_dot(h, wdq_ref[...]), qn_ref[...]).astype(BF16)
    q = _dot(cq, wuq_ref[...])
    hw = MLA_H * LANES
    for hd in range(MLA_H):
        qn = q[:, hd * MLA_NOPE:(hd + 1) * MLA_NOPE].astype(BF16)
        qa = _dot_nt(qn, wuk_ref[hd])
        roped = (q[:, hw + hd * LANES:hw + (hd + 1) * LANES] * cos
                 + q[:, 2 * hw + hd * LANES:2 * hw + (hd + 1) * LANES] * sin)
        yield hd, qa, roped[:, :MLA_ROPE]


def _lane_fold(x, op):
    parts = [x[:, c:c + LANES] for c in range(0, x.shape[1], LANES)]
    while len(parts) > 1:
        parts = [op(parts[i], parts[i + 1]) for i in range(0, len(parts) - 1, 2)] + parts[len(parts) & ~1:]
    return parts[0]


def _lane_tile(x, width):
    return x if width == LANES else jnp.concatenate([x] * (width // LANES), axis=1)


def _softmax_step(s, pv_fn, m_ref, l_ref, acc_ref, rows):
    m_prev = m_ref[rows, :]
    m_new = jnp.maximum(m_prev, jnp.max(_lane_fold(s, jnp.maximum), axis=1, keepdims=True))
    alpha = jnp.exp2(m_prev - m_new)
    p = jnp.exp2(s - _lane_tile(m_new, s.shape[1]))
    l_ref[rows, :] = alpha * l_ref[rows, :] + _lane_fold(p, jnp.add)
    acc_ref[rows, :] = _lane_tile(alpha, acc_ref.shape[1]) * acc_ref[rows, :] + pv_fn(p.astype(BF16))
    m_ref[rows, :] = m_new


def _softmax_result(l_ref, acc_ref, rows):
    return acc_ref[rows, :] / jnp.sum(l_ref[rows, :], axis=1, keepdims=True)


def _mla_out_heads(x, ol_fn, oc_ref, wuv_ref, wo_ref, gpost_ref):
    for hd in range(MLA_H):
        oc_ref[:, hd * MLA_V:(hd + 1) * MLA_V] = _dot(ol_fn(hd).astype(BF16), wuv_ref[hd])
    mix = _dot(oc_ref[...].astype(BF16), wo_ref[...])
    return x + _rms(mix, gpost_ref[...])


def _mla_prompt_kernel(x_ref, ckv_ref, kr_ref, cos_ref, sin_ref, gpre_ref, wdq_ref, qn_ref, wuq_ref,
                       wuk_ref, wuv_ref, wo_ref, gpost_ref, y_ref,
                       qa_ref, qr_ref, m_ref, l_ref, acc_ref, oc_ref, *, tq):
    i = pl.program_id(1)
    x = x_ref[...]
    for hd, qa, qr in _mla_q_heads(x, cos_ref[...], sin_ref[...], gpre_ref, wdq_ref, qn_ref, wuq_ref, wuk_ref):
        qa_ref[hd * tq:(hd + 1) * tq, :] = qa.astype(BF16)
        qr_ref[hd * tq:(hd + 1) * tq, :] = qr.astype(BF16)
    m_ref[...] = jnp.full(m_ref.shape, -jnp.inf, F32)
    l_ref[...] = jnp.zeros(l_ref.shape, F32)
    acc_ref[...] = jnp.zeros(acc_ref.shape, F32)

    gm = MLA_HEAD_GROUP * tq

    def key_block(j, causal):
        k0 = pl.multiple_of(j * tq, tq)
        kc = ckv_ref[pl.ds(k0, tq), :]
        kk = kr_ref[pl.ds(k0, tq), :]
        if causal:
            visible = (lax.broadcasted_iota(jnp.int32, (gm, tq), 1)
                       <= (lax.broadcasted_iota(jnp.int32, (gm, tq), 0) & (tq - 1)))
        for g in range(MLA_H // MLA_HEAD_GROUP):
            rows = slice(g * gm, (g + 1) * gm)
            s = (_dot_nt(qa_ref[rows, :], kc) + _dot_nt(qr_ref[rows, :], kk)) * SM_SCALE_LOG2
            if causal:
                s = jnp.where(visible, s, -jnp.inf)
            _softmax_step(s, lambda p: _dot(p, kc), m_ref, l_ref, acc_ref, rows)

    def body(j, carry):
        key_block(j, False)
        return carry

    lax.fori_loop(0, i, body, 0)
    key_block(i, True)

    def ol_fn(hd):
        return _softmax_result(l_ref, acc_ref, slice(hd * tq, (hd + 1) * tq))

    y_ref[...] = _mla_out_heads(x, ol_fn, oc_ref, wuv_ref, wo_ref, gpost_ref)


def _mla_weight_specs(w, keys):
    return [_const_spec(w[k].shape) for k in keys]


MLA_HEAD_GROUP = 8
MLA_Q_KEYS = ("gpre", "wdq", "qnorm", "wuq", "wuk")
MLA_O_KEYS = ("wuv", "wo", "gpost")


def _mla_prompt(x, ckv_b, kr_b, cos, sin, w, *, tq, seq):
    rows, d = x.shape
    nt = seq // tq
    assert rows % seq == 0 and seq % tq == 0
    m = MLA_H * tq
    x_spec = pl.BlockSpec((tq, d), lambda b, i: (b * nt + i, 0))
    t_spec = pl.BlockSpec((tq, LANES), lambda b, i: (i, 0))
    keys = MLA_Q_KEYS + MLA_O_KEYS
    return pl.pallas_call(
        functools.partial(_mla_prompt_kernel, tq=tq),
        grid=(rows // seq, nt),
        in_specs=[x_spec,
                  pl.BlockSpec((seq, MLA_KV_RANK), lambda b, i: (b, 0)),
                  pl.BlockSpec((seq, MLA_ROPE), lambda b, i: (b, 0)),
                  t_spec, t_spec] + _mla_weight_specs(w, keys),
        out_specs=x_spec,
        out_shape=jax.ShapeDtypeStruct(x.shape, F32),
        scratch_shapes=[pltpu.VMEM((m, MLA_KV_RANK), BF16), pltpu.VMEM((m, MLA_ROPE), BF16),
                        pltpu.VMEM((m, LANES), F32), pltpu.VMEM((m, LANES), F32), pltpu.VMEM((m, MLA_KV_RANK), F32),
                        pltpu.VMEM((tq, MLA_H * MLA_V), F32)],
        compiler_params=_params(2),
        name="mla_prompt",
    )(x, ckv_b, kr_b, cos, sin, *[w[k] for k in keys])


def _mla_q_kernel(x_ref, cos_ref, sin_ref, gpre_ref, wdq_ref, qn_ref, wuq_ref, wuk_ref, qa_ref, qr_ref):
    for hd, qa, qr in _mla_q_heads(x_ref[...], cos_ref[...], sin_ref[...], gpre_ref, wdq_ref, qn_ref,
                                   wuq_ref, wuk_ref):
        qa_ref[hd] = qa
        qr_ref[hd] = qr


def _mla_q(x, cos, sin, w, *, tm):
    rows, d = x.shape
    assert rows % tm == 0
    return pl.pallas_call(
        _mla_q_kernel,
        grid=(rows // tm,),
        in_specs=[pl.BlockSpec((tm, d), lambda i: (i, 0)), pl.BlockSpec((tm, LANES), lambda i: (i, 0)),
                  pl.BlockSpec((tm, LANES), lambda i: (i, 0))] + _mla_weight_specs(w, MLA_Q_KEYS),
        out_specs=[pl.BlockSpec((MLA_H, tm, MLA_KV_RANK), lambda i: (0, i, 0)),
                   pl.BlockSpec((MLA_H, tm, MLA_ROPE), lambda i: (0, i, 0))],
        out_shape=[jax.ShapeDtypeStruct((MLA_H, rows, MLA_KV_RANK), F32),
                   jax.ShapeDtypeStruct((MLA_H, rows, MLA_ROPE), F32)],
        compiler_params=_params(1),
        name="mla_q",
    )(x, cos, sin, *[w[k] for k in MLA_Q_KEYS])


def _mla_attend_sample_kernel(pt_ref, qa_ref, qr_ref, ckvn_ref, krn_ref, ckv_hbm, krt_hbm, ol_ref,
                              ckv_buf, krt_buf, sem, m_ref, l_ref, acc_ref, *, tq, pages):
    b = pl.program_id(0)
    j = pl.program_id(1)
    nb = pl.num_programs(0)
    nj = pl.num_programs(1)
    step = b * nj + j
    slot = step % 2
    every = slice(None)
    n_streams = m_ref.shape[0]
    per_stream = pages // n_streams

    def page_copies(bb, jj, sl):
        out = []
        for i in range(pages):
            pid = pt_ref[bb, jj * pages + i]
            out.append(pltpu.make_async_copy(ckv_hbm.at[pid], ckv_buf.at[sl, i], sem.at[sl]))
            out.append(pltpu.make_async_copy(krt_hbm.at[pid], krt_buf.at[sl, i], sem.at[sl]))
        return out

    @pl.when(step == 0)
    def _():
        for c in page_copies(b, j, slot):
            c.start()

    @pl.when(step + 1 < nb * nj)
    def _():
        wrap = j + 1 == nj
        for c in page_copies(jnp.where(wrap, b + 1, b), jnp.where(wrap, 0, j + 1), 1 - slot):
            c.start()

    @pl.when(j == 0)
    def _():
        m_ref[...] = jnp.full(m_ref.shape, -jnp.inf, F32)
        l_ref[...] = jnp.zeros(l_ref.shape, F32)
        acc_ref[...] = jnp.zeros(acc_ref.shape, F32)

    qa = qa_ref[0].astype(BF16)
    qr = qr_ref[0].astype(BF16)
    for c in page_copies(b, j, slot):
        c.wait()
    for st in range(n_streams):
        ids = range(st * per_stream, (st + 1) * per_stream)
        kcs = [ckv_buf[slot, i].astype(BF16) for i in ids]
        s = jnp.concatenate([_dot_nt(qa, kc) + _dot(qr, krt_buf[slot, i].astype(BF16)) for i, kc in zip(ids, kcs)],
                            axis=1) * SM_SCALE_LOG2
        page = kcs[0].shape[0]

        def pv_pages(p, kcs=kcs):
            out = _dot(p[:, :page], kcs[0])
            for i in range(1, len(kcs)):
                out = out + _dot(p[:, i * page:(i + 1) * page], kcs[i])
            return out

        _softmax_step(s, pv_pages, m_ref.at[st], l_ref.at[st], acc_ref.at[st], every)

    @pl.when(j == pl.num_programs(1) - 1)
    def _():
        kc = ckvn_ref[0].astype(BF16)
        kk = krn_ref[0].astype(BF16)
        n_new = kc.shape[0]
        qpos = lax.broadcasted_iota(jnp.int32, (qa.shape[0], 1), 0) & (tq - 1)
        kpos = lax.broadcasted_iota(jnp.int32, (1, n_new), 1)
        sn = jnp.where(kpos <= qpos, (_dot_nt(qa, kc) + _dot_nt(qr, kk)) * SM_SCALE_LOG2, -jnp.inf)
        _softmax_step(sn, lambda p: _dot(p, kc), m_ref.at[0], l_ref.at[0], acc_ref.at[0], every)
        m_all = m_ref[0]
        for st in range(1, n_streams):
            m_all = jnp.maximum(m_all, m_ref[st])
        l_all = jnp.zeros(l_ref.shape[1:], F32)
        acc_all = jnp.zeros(acc_ref.shape[1:], F32)
        for st in range(n_streams):
            w_st = jnp.exp2(m_ref[st] - m_all)
            l_all = l_all + w_st * l_ref[st]
            acc_all = acc_all + _lane_tile(w_st, acc_all.shape[1]) * acc_ref[st]
        ol_ref[0] = acc_all / jnp.sum(l_all, axis=1, keepdims=True)


SAMPLE_SOFTMAX_STREAMS = 2


def _mla_attend_sample(qa, qr, ckv_new, kr_new, cache_ckv, cache_krt, page_table, *, tq, pages):
    bsz, n_pages = page_table.shape
    page = cache_ckv.shape[1]
    assert n_pages % pages == 0 and pages % SAMPLE_SOFTMAX_STREAMS == 0
    assert tq & (tq - 1) == 0 and ckv_new.shape[1] == page
    m = MLA_H * tq
    q_spec = pl.BlockSpec((1, m, MLA_KV_RANK), lambda b, j, pt: (b, 0, 0))
    r_spec = pl.BlockSpec((1, m, MLA_ROPE), lambda b, j, pt: (b, 0, 0))
    hbm_spec = pl.BlockSpec(memory_space=pl.ANY)
    grid_spec = pltpu.PrefetchScalarGridSpec(
        num_scalar_prefetch=1,
        grid=(bsz, n_pages // pages),
        in_specs=[q_spec, r_spec,
                  pl.BlockSpec((1, page, MLA_KV_RANK), lambda b, j, pt: (b, 0, 0)),
                  pl.BlockSpec((1, page, MLA_ROPE), lambda b, j, pt: (b, 0, 0)),
                  hbm_spec, hbm_spec],
        out_specs=q_spec,
        scratch_shapes=[pltpu.VMEM((2, pages, page, MLA_KV_RANK), cache_ckv.dtype),
                        pltpu.VMEM((2, pages, MLA_ROPE, page), cache_krt.dtype),
                        pltpu.SemaphoreType.DMA((2,)),
                        pltpu.VMEM((SAMPLE_SOFTMAX_STREAMS, m, LANES), F32),
                        pltpu.VMEM((SAMPLE_SOFTMAX_STREAMS, m, LANES), F32),
                        pltpu.VMEM((SAMPLE_SOFTMAX_STREAMS, m, MLA_KV_RANK), F32)],
    )
    return pl.pallas_call(
        functools.partial(_mla_attend_sample_kernel, tq=tq, pages=pages),
        grid_spec=grid_spec,
        out_shape=jax.ShapeDtypeStruct(qa.shape, F32),
        compiler_params=_params(2),
        name="mla_attend_sample",
    )(page_table, qa, qr, ckv_new, kr_new, cache_ckv, cache_krt)


def _mla_out_kernel(x_ref, ol_ref, wuv_ref, wo_ref, gpost_ref, y_ref, oc_ref):
    y_ref[...] = _mla_out_heads(x_ref[...], lambda hd: ol_ref[hd], oc_ref, wuv_ref, wo_ref, gpost_ref)


def _mla_out(x, ol, w, *, tm):
    rows, d = x.shape
    assert rows % tm == 0
    x_spec = pl.BlockSpec((tm, d), lambda i: (i, 0))
    return pl.pallas_call(
        _mla_out_kernel,
        grid=(rows // tm,),
        in_specs=[x_spec, pl.BlockSpec((MLA_H, tm, MLA_KV_RANK), lambda i: (0, i, 0))]
        + _mla_weight_specs(w, MLA_O_KEYS),
        out_specs=x_spec,
        out_shape=jax.ShapeDtypeStruct(x.shape, F32),
        scratch_shapes=[pltpu.VMEM((tm, MLA_H * MLA_V), F32)],
        compiler_params=_params(1),
        name="mla_out",
    )(x, ol, *[w[k] for k in MLA_O_KEYS])


def _row(v):
    return v.reshape(1, -1).astype(F32)


def _rot_cols(w):
    half = w.shape[-1] // 2
    return jnp.concatenate([-w[..., half:], w[..., :half]], axis=-1)


def _prep_weights(p):
    gla, mla = [], []
    for l in range(N_GLA):
        rank = p["gla_w_gk1"].shape[-1]
        gla.append(dict(
            gpre=_row(p["norm_mix_pre"][l]), gpost=_row(p["norm_mix_post"][l]),
            wqkvr=jnp.concatenate([p["gla_w_q"][l], p["gla_w_k"][l], p["gla_w_v"][l], p["gla_w_r"][l]],
                                  axis=1).astype(BF16),
            wg1=jnp.pad(p["gla_w_gk1"][l], ((0, 0), (0, LANES - rank))).astype(BF16),
            wg2=jnp.pad(p["gla_w_gk2"][l], ((0, LANES - rank), (0, 0))).astype(BF16),
            bgk=_row(p["gla_b_gk"][l]), gnorm=_row(p["gla_norm"][l]), wo=p["gla_w_o"][l].astype(BF16)))
    ffn = dict(
        gpre=p["norm_ffn_pre"][:, None, :].astype(F32), gpost=p["norm_ffn_post"][:, None, :].astype(F32),
        wg=p["ffn_w_gate"].astype(BF16), wu=p["ffn_w_up"].astype(BF16),
        cw=p["ffn_conv_w"].astype(F32), cb=p["ffn_conv_b"][:, None, :].astype(F32),
        wd=p["ffn_w_down"].astype(BF16))
    wuk = jnp.transpose(p["mla_w_uk"], (1, 0, 2)).astype(BF16)
    wuv = jnp.transpose(p["mla_w_uv"], (1, 0, 2)).astype(BF16)
    for j in range(DEPTH - N_GLA):
        l = N_GLA + j
        wuq = p["mla_w_uq"][j].reshape(MLA_Q_RANK, MLA_H, MLA_NOPE + MLA_ROPE)
        nope = wuq[:, :, :MLA_NOPE].reshape(MLA_Q_RANK, MLA_H * MLA_NOPE)
        rope = wuq[:, :, MLA_NOPE:]
        lane_pad = ((0, 0), (0, 0), (0, LANES - MLA_ROPE))
        rope_p = jnp.pad(rope, lane_pad).reshape(MLA_Q_RANK, MLA_H * LANES)
        rot_p = jnp.pad(_rot_cols(rope), lane_pad).reshape(MLA_Q_RANK, MLA_H * LANES)
        mla.append(dict(
            gpre=_row(p["norm_mix_pre"][l]), gpost=_row(p["norm_mix_post"][l]),
            wdq=p["mla_w_dq"][j].astype(BF16), qnorm=_row(p["mla_q_norm"][j]),
            wuq=jnp.concatenate([nope, rope_p, rot_p], axis=1).astype(BF16),
            wuk=wuk, wuv=wuv, wo=p["mla_w_o"][j].astype(BF16)))
    kv = dict(gsrc=_row(p["mla_kv_src_norm"]), wdkv=p["mla_w_dkv"].astype(BF16), gkv=_row(p["mla_kv_norm"]),
              wkr=p["mla_w_kr"].astype(BF16), wkrot=_rot_cols(p["mla_w_kr"]).astype(BF16))
    return gla, ffn, mla, kv


def _rope_tables(pos):
    half = MLA_ROPE // 2
    inv = ROPE_THETA ** (-jnp.arange(half, dtype=F32) / half)
    ang = pos.astype(F32)[:, None] * inv[None, :]
    cos, sin = jnp.cos(ang), jnp.sin(ang)
    return jnp.concatenate([cos, cos], axis=-1), jnp.concatenate([sin, sin], axis=-1)


def _trunk_prompt(x, weights):
    gla_w, ffn_w, mla_w, kv_w = weights
    bsz, t, d = x.shape
    x = x.reshape(bsz * t, d)
    cos, sin = _rope_tables(jnp.arange(t, dtype=jnp.int32))
    cos2, sin2 = jnp.concatenate([cos, cos], -1), jnp.concatenate([sin, sin], -1)
    s0 = jnp.zeros((1, bsz, GLA_H, GLA_DK, GLA_DV), F32)
    buf0 = jnp.zeros((bsz, CONV_W - 1, FFN_DIM), F32)
    tm_gla, tm_ffn, tm_kv, tq = 256, 512, 512, 256
    states, bufs = [], []
    ckv = kr = ckv_b = kr_b = None
    for l in range(DEPTH):
        if l < N_GLA:
            x, s = _gla_layer(x, s0, 0, gla_w[l], tm=tm_gla, cp=GLA_CHUNK, n_valid=GLA_CHUNK,
                              tiles_per_state=t // tm_gla)
            states.append(s)
        else:
            if l == N_GLA:
                ckv, kr, ckv_b, kr_b = _kv_proj(x, cos, sin, kv_w, tm=tm_kv)
            x = _mla_prompt(x, ckv_b, kr_b, cos2, sin2, mla_w[l - N_GLA], tq=tq, seq=t)
        x, b = _ffn_seq(x, buf0, ffn_w, l, tm=tm_ffn, tiles_per_seq=t // tm_ffn)
        bufs.append(b)
    return (x.reshape(bsz, t, d), jnp.stack(states), jnp.stack(bufs),
            ckv.reshape(bsz, t, -1), kr.reshape(bsz, t, -1))


def _heads_to_seq(q, bsz, t):
    h, _, n = q.shape
    return q.reshape(h, bsz, -1, n)[:, :, :t].transpose(1, 0, 2, 3).reshape(bsz, h * t, n)


def _seq_to_heads(o, tp):
    bsz, m, n = o.shape
    t = m // MLA_H
    o = o.reshape(bsz, MLA_H, t, n).transpose(1, 0, 2, 3)
    return jnp.pad(o, ((0, 0), (0, 0), (0, tp - t), (0, 0))).reshape(MLA_H, bsz * tp, n)


def _trunk_sample(x, state_gla, state_conv, cache_ckv, cache_kr, page_table, weights):
    gla_w, ffn_w, mla_w, kv_w = weights
    bsz, t, d = x.shape
    tp = -(-t // SUBLANES) * SUBLANES
    assert tp & (tp - 1) == 0 and t >= CONV_W - 1
    rows = bsz * tp
    page = cache_ckv.shape[1]
    past = page_table.shape[1] * page
    x = jnp.pad(x, ((0, 0), (0, tp - t), (0, 0))).reshape(rows, d)
    cos, sin = _rope_tables(past + jnp.arange(tp, dtype=jnp.int32))
    cos, sin = jnp.tile(cos, (bsz, 1)), jnp.tile(sin, (bsz, 1))
    cos2, sin2 = jnp.concatenate([cos, cos], -1), jnp.concatenate([sin, sin], -1)
    cache_krt = jnp.swapaxes(cache_kr, 1, 2)
    tm_gla = 8 * tp
    pad_tail = ((0, 0), (0, tp - (CONV_W - 1)), (0, 0))
    states, bufs = [], []
    ckv = kr = ckv_pad = kr_pad = None
    for l in range(DEPTH):
        if l < N_GLA:
            x, s = _gla_layer(x, state_gla, l, gla_w[l], tm=tm_gla, cp=tp, n_valid=t, tiles_per_state=0)
            states.append(s)
        else:
            w = mla_w[l - N_GLA]
            if l == N_GLA:
                ckv, kr, _, _ = _kv_proj(x, cos, sin, kv_w, tm=rows)
                ckv = ckv.reshape(bsz, tp, -1)
                kr = kr.reshape(bsz, tp, -1)
                ckv_pad = jnp.pad(ckv, ((0, 0), (0, page - tp), (0, 0)))
                kr_pad = jnp.pad(kr, ((0, 0), (0, page - tp), (0, 0)))
            qa, qr = _mla_q(x, cos2, sin2, w, tm=rows)
            ol = _mla_attend_sample(_heads_to_seq(qa, bsz, t), _heads_to_seq(qr, bsz, t), ckv_pad, kr_pad,
                                    cache_ckv, cache_krt, page_table, tq=t, pages=32)
            x = _mla_out(x, _seq_to_heads(ol, tp), w, tm=rows)
        buf = state_conv[l]
        prev1 = jnp.pad(buf[:, CONV_W - 2:], ((0, 0), (0, tp - 1), (0, 0))).reshape(rows, FFN_DIM)
        prev2 = jnp.pad(buf, pad_tail).reshape(rows, FFN_DIM)
        x, u = _ffn_rows(x, prev1, prev2, ffn_w, l, tm=rows, tp=tp)
        bufs.append(u.reshape(bsz, tp, FFN_DIM)[:, t - (CONV_W - 1):t])
    return (x.reshape(bsz, tp, d)[:, :t], jnp.stack(states), jnp.stack(bufs), ckv[:, :t], kr[:, :t])


def kernel(x_prompt, x_sample, state_gla, state_ffn_conv, cache_ckv, cache_krope, page_table, norm_mix_pre, norm_mix_post, norm_ffn_pre, norm_ffn_post, ffn_w_gate, ffn_w_up, ffn_conv_w, ffn_conv_b, ffn_w_down, gla_w_q, gla_w_k, gla_w_v, gla_w_gk1, gla_w_gk2, gla_b_gk, gla_w_r, gla_norm, gla_w_o, mla_kv_src_norm, mla_w_dkv, mla_kv_norm, mla_w_kr, mla_w_uk, mla_w_uv, mla_w_dq, mla_q_norm, mla_w_uq, mla_w_o):
    p = dict(norm_mix_pre=norm_mix_pre, norm_mix_post=norm_mix_post, norm_ffn_pre=norm_ffn_pre,
             norm_ffn_post=norm_ffn_post, ffn_w_gate=ffn_w_gate, ffn_w_up=ffn_w_up, ffn_conv_w=ffn_conv_w,
             ffn_conv_b=ffn_conv_b, ffn_w_down=ffn_w_down, gla_w_q=gla_w_q, gla_w_k=gla_w_k, gla_w_v=gla_w_v,
             gla_w_gk1=gla_w_gk1, gla_w_gk2=gla_w_gk2, gla_b_gk=gla_b_gk, gla_w_r=gla_w_r, gla_norm=gla_norm,
             gla_w_o=gla_w_o, mla_kv_src_norm=mla_kv_src_norm, mla_w_dkv=mla_w_dkv, mla_kv_norm=mla_kv_norm,
             mla_w_kr=mla_w_kr, mla_w_uk=mla_w_uk, mla_w_uv=mla_w_uv, mla_w_dq=mla_w_dq, mla_q_norm=mla_q_norm,
             mla_w_uq=mla_w_uq, mla_w_o=mla_w_o)
    weights = _prep_weights(p)
    y_p, gla_p, conv_p, ckv_p, kr_p = _trunk_prompt(x_prompt, weights)
    y_s, gla_s, conv_s, ckv_s, kr_s = _trunk_sample(x_sample, state_gla, state_ffn_conv, cache_ckv,
                                                    cache_krope, page_table, weights)
    return (y_p, y_s, gla_p, gla_s, conv_p, conv_s, ckv_p, ckv_s, kr_p, kr_s)
```

```python
import functools

import jax
import jax.numpy as jnp
from jax import lax
from jax.experimental import pallas as pl
from jax.experimental.pallas import tpu as pltpu

D_MODEL = 1024
DEPTH = 4
N_GLA = DEPTH // 2
GLA_H = 4
GLA_DK = 128
GLA_DV = 256
GLA_GATE_TEMP = 16.0
GLA_CHUNK = 64
GLA_SUBTILE = 256
MLA_H = 8
MLA_NOPE = 128
MLA_ROPE = 64
MLA_V = 128
MLA_Q_RANK = 384
MLA_KV_RANK = 256
ROPE_THETA = 10000.0
FFN_DIM = 2816
CONV_W = 3
EPS = 1e-6
LANES = 128
SUBLANES = 8
MXU_DIM = 256
VMEM_LIMIT = 56 * 1024 * 1024

F32 = jnp.float32
BF16 = jnp.bfloat16
NT_DIMS = (((1,), (1,)), ((), ()))
TN_DIMS = (((0,), (0,)), ((), ()))
LOG2_E = 1.4426950408889634
SM_SCALE_LOG2 = (MLA_NOPE + MLA_ROPE) ** -0.5 * LOG2_E


def _rms(x, g):
    return x * lax.rsqrt(jnp.mean(x * x, axis=-1, keepdims=True) + EPS) * g


def _dot(a, b):
    return jnp.dot(a, b, preferred_element_type=F32)


def _dot_nt(a, b):
    return lax.dot_general(a, b, NT_DIMS, preferred_element_type=F32)


def _log_sigmoid(z):
    return jnp.minimum(z, 0.0) - jnp.log1p(jnp.exp(-jnp.abs(z)))


def _gelu_tanh(x):
    cdf = 0.5 * (1.0 + jnp.tanh(0.7978845608028654 * (x + 0.044715 * (x * x * x))))
    return x * cdf


def _const_spec(shape):
    nd = len(shape)
    return pl.BlockSpec(shape, lambda *_: (0,) * nd, pipeline_mode=pl.Buffered(1))


def _params(n_axes):
    return pltpu.CompilerParams(dimension_semantics=("arbitrary",) * n_axes,
                                vmem_limit_bytes=VMEM_LIMIT)


def _gla_kernel(x_ref, s0_ref, gpre_ref, wqkvr_ref, wg1_ref, wg2_ref, bgk_ref, gn_ref, wo_ref,
                gpost_ref, y_ref, s_ref, proj_ref, lg_ref, o_ref, *, tm, cp, n_valid, tiles_per_state):
    if tiles_per_state:
        @pl.when(pl.program_id(0) % tiles_per_state == 0)
        def _():
            s_ref[...] = s0_ref[...]
    else:
        s_ref[...] = s0_ref[...]

    x = x_ref[...]
    h = _rms(x, gpre_ref[...]).astype(BF16)
    g1 = _dot(h, wg1_ref[...])
    z = _dot(g1.astype(BF16), wg2_ref[...]) + bgk_ref[...]
    lg = _log_sigmoid(z) / GLA_GATE_TEMP
    if n_valid < cp:
        t_in_chunk = lax.broadcasted_iota(jnp.int32, (tm, 1), 0) & (cp - 1)
        lg = jnp.where(t_in_chunk < n_valid, lg, 0.0)
    lg_ref[...] = lg
    proj_ref[...] = _dot(h, wqkvr_ref[...])

    ts = min(tm, GLA_SUBTILE)
    nc = ts // cp
    r_i = lax.broadcasted_iota(jnp.int32, (ts, ts), 0)
    c_i = lax.broadcasted_iota(jnp.int32, (ts, ts), 1)
    intra = (r_i >= c_i) & ((r_i & -cp) == (c_i & -cp))
    t_in = lax.broadcasted_iota(jnp.int32, (ts, GLA_DK), 0) & (cp - 1)
    eye = (lax.broadcasted_iota(jnp.int32, (GLA_DK, GLA_DK), 0)
           == lax.broadcasted_iota(jnp.int32, (GLA_DK, GLA_DK), 1))
    k_off = GLA_H * GLA_DK
    v_off = 2 * GLA_H * GLA_DK
    r_off = v_off + GLA_H * GLA_DV

    for hd in range(GLA_H):
        kcol = slice(hd * GLA_DK, (hd + 1) * GLA_DK)
        s = s_ref[0, hd]
        for sub in range(tm // ts):
            r0 = sub * ts
            b = lg_ref[r0:r0 + ts, kcol]
            shift = 1
            while shift < cp:
                b = b + jnp.where(t_in >= shift, pltpu.roll(b, shift, 0), 0.0)
                shift *= 2
            bl = jnp.concatenate(
                [jnp.broadcast_to(b[(c + 1) * cp - 1:(c + 1) * cp, :], (cp, GLA_DK)) for c in range(nc)], axis=0)
            q = proj_ref[r0:r0 + ts, hd * GLA_DK:(hd + 1) * GLA_DK] * (GLA_DK ** -0.5)
            k = proj_ref[r0:r0 + ts, k_off + hd * GLA_DK:k_off + (hd + 1) * GLA_DK]
            v = proj_ref[r0:r0 + ts, v_off + hd * GLA_DV:v_off + (hd + 1) * GLA_DV]
            qe = q * jnp.exp(b)
            ke = k * jnp.exp(-b)
            kd = k * jnp.exp(bl - b)
            a = jnp.where(intra, _dot_nt(qe.astype(BF16), ke.astype(BF16)), 0.0).astype(BF16)
            o_intra = _dot(a, v.astype(BF16))
            for c in range(nc):
                rows = slice(c * cp, (c + 1) * cp)
                si = sub * nc + c
                if not tiles_per_state:
                    s = s_ref[si, hd]
                o_ref[r0 + c * cp:r0 + (c + 1) * cp, hd * GLA_DV:(hd + 1) * GLA_DV] = (
                    o_intra[rows, :] + _dot(qe[rows, :].astype(BF16), s.astype(BF16)))
                dcol = jnp.sum(jnp.where(eye, jnp.exp(bl[c * cp:c * cp + 1, :]), 0.0), axis=1, keepdims=True)
                s = dcol * s + lax.dot_general(kd[rows, :].astype(BF16), v[rows, :].astype(BF16), TN_DIMS,
                                               preferred_element_type=F32)
                if not tiles_per_state:
                    s_ref[si, hd] = s
        if tiles_per_state:
            s_ref[0, hd] = s

    gn = gn_ref[...]
    for hd in range(GLA_H):
        vcol = slice(hd * GLA_DV, (hd + 1) * GLA_DV)
        r = proj_ref[:, r_off + hd * GLA_DV:r_off + (hd + 1) * GLA_DV]
        o_ref[:, vcol] = _rms(o_ref[:, vcol], gn) * (r * jax.nn.sigmoid(r))
    mix = _dot(o_ref[...].astype(BF16), wo_ref[...])
    y_ref[...] = x + _rms(mix, gpost_ref[...])


def _gla_layer(x, s0_all, layer, w, *, tm, cp, n_valid, tiles_per_state):
    rows, d = x.shape
    assert rows % tm == 0 and tm % cp == 0 and cp & (cp - 1) == 0
    n_proj = w["wqkvr"].shape[1]
    x_spec = pl.BlockSpec((tm, d), lambda i: (i, 0))
    n_blk = 1 if tiles_per_state else tm // cp
    per = tiles_per_state or 1
    s_in_spec = pl.BlockSpec((None, n_blk, GLA_H, GLA_DK, GLA_DV), lambda i: (layer, i // per, 0, 0, 0))
    s_spec = pl.BlockSpec((n_blk, GLA_H, GLA_DK, GLA_DV), lambda i: (i // per, 0, 0, 0))
    return pl.pallas_call(
        functools.partial(_gla_kernel, tm=tm, cp=cp, n_valid=n_valid, tiles_per_state=tiles_per_state),
        grid=(rows // tm,),
        in_specs=[x_spec, s_in_spec, _const_spec((1, d)), _const_spec(w["wqkvr"].shape),
                  _const_spec(w["wg1"].shape), _const_spec(w["wg2"].shape), _const_spec(w["bgk"].shape),
                  _const_spec(w["gnorm"].shape), _const_spec(w["wo"].shape), _const_spec((1, d))],
        out_specs=[x_spec, s_spec],
        out_shape=[jax.ShapeDtypeStruct(x.shape, F32), jax.ShapeDtypeStruct(s0_all.shape[1:], F32)],
        scratch_shapes=[pltpu.VMEM((tm, n_proj), F32), pltpu.VMEM((tm, GLA_H * GLA_DK), F32),
                        pltpu.VMEM((tm, GLA_H * GLA_DV), F32)],
        compiler_params=_params(1),
        name="gla_layer",
    )(x, s0_all, w["gpre"], w["wqkvr"], w["wg1"], w["wg2"], w["bgk"], w["gnorm"], w["wo"], w["gpost"])


def _ffn_core(x, gpre_ref, wg_ref, wu_ref, cw_ref, cb_ref, wd_ref, gpost_ref, shifted, emit_u):
    cut = (FFN_DIM // MXU_DIM + 1) // 2 * MXU_DIM
    assert FFN_DIM % MXU_DIM == 0
    h = _rms(x, gpre_ref[...]).astype(BF16)
    acc = jnp.zeros(x.shape, F32)
    for sl in (slice(0, cut), slice(cut, FFN_DIM)):
        u = _dot(h, wg_ref[:, sl])
        up = _dot(h, wu_ref[:, sl])
        u1, u2 = shifted(u, sl)
        c = cb_ref[:, sl] + ((cw_ref[0:1, sl] * u2 + cw_ref[1:2, sl] * u1) + cw_ref[2:3, sl] * u)
        act = (_gelu_tanh(c) * up).astype(BF16)
        acc = acc + _dot(act, wd_ref[sl, :])
        emit_u(u, sl)
    return x + _rms(acc, gpost_ref[...])


def _ffn_seq_kernel(x_ref, buf_ref, gpre_ref, wg_ref, wu_ref, cw_ref, cb_ref, wd_ref, gpost_ref,
                    y_ref, bufn_ref, carry_ref, *, tm, tiles_per_seq):
    @pl.when(pl.program_id(0) % tiles_per_seq == 0)
    def _():
        carry_ref[...] = buf_ref[0]

    row = lax.broadcasted_iota(jnp.int32, (tm, 1), 0)

    def shifted(u, sl):
        p0 = carry_ref[0:1, sl]
        p1 = carry_ref[1:2, sl]
        u1 = jnp.where(row == 0, p1, pltpu.roll(u, 1, 0))
        u2 = jnp.where(row == 0, p0, jnp.where(row == 1, p1, pltpu.roll(u, 2, 0)))
        return u1, u2

    def emit_u(u, sl):
        tail = u[tm - (CONV_W - 1):tm, :]
        carry_ref[:, sl] = tail
        bufn_ref[0, :, sl] = tail

    y_ref[...] = _ffn_core(x_ref[...], gpre_ref, wg_ref, wu_ref, cw_ref, cb_ref, wd_ref, gpost_ref,
                           shifted, emit_u)


def _ffn_rows_kernel(x_ref, prev1_ref, prev2_ref, gpre_ref, wg_ref, wu_ref, cw_ref, cb_ref, wd_ref,
                     gpost_ref, y_ref, u_ref, *, tm, tp):
    t = lax.broadcasted_iota(jnp.int32, (tm, 1), 0) & (tp - 1)

    def shifted(u, sl):
        u1 = jnp.where(t == 0, prev1_ref[:, sl], pltpu.roll(u, 1, 0))
        u2 = jnp.where(t < 2, prev2_ref[:, sl], pltpu.roll(u, 2, 0))
        return u1, u2

    def emit_u(u, sl):
        u_ref[:, sl] = u

    y_ref[...] = _ffn_core(x_ref[...], gpre_ref, wg_ref, wu_ref, cw_ref, cb_ref, wd_ref, gpost_ref,
                           shifted, emit_u)


FFN_KEYS = ("gpre", "wg", "wu", "cw", "cb", "wd", "gpost")


def _ffn_weight_specs(w, layer):
    return [pl.BlockSpec((None,) + w[k].shape[1:], lambda *_: (layer, 0, 0), pipeline_mode=pl.Buffered(1))
            for k in FFN_KEYS]


def _ffn_weights(w):
    return [w[k] for k in FFN_KEYS]


def _ffn_seq(x, buf, w, layer, *, tm, tiles_per_seq):
    rows, d = x.shape
    assert rows % (tm * tiles_per_seq) == 0 and tm >= CONV_W - 1
    x_spec = pl.BlockSpec((tm, d), lambda i: (i, 0))
    b_spec = pl.BlockSpec((1, CONV_W - 1, FFN_DIM), lambda i: (i // tiles_per_seq, 0, 0))
    return pl.pallas_call(
        functools.partial(_ffn_seq_kernel, tm=tm, tiles_per_seq=tiles_per_seq),
        grid=(rows // tm,),
        in_specs=[x_spec, b_spec] + _ffn_weight_specs(w, layer),
        out_specs=[x_spec, b_spec],
        out_shape=[jax.ShapeDtypeStruct(x.shape, F32), jax.ShapeDtypeStruct(buf.shape, F32)],
        scratch_shapes=[pltpu.VMEM((CONV_W - 1, FFN_DIM), F32)],
        compiler_params=_params(1),
        name="ffn_seq",
    )(x, buf, *_ffn_weights(w))


def _ffn_rows(x, prev1, prev2, w, layer, *, tm, tp):
    rows, d = x.shape
    assert rows % tm == 0 and tm % tp == 0 and tp & (tp - 1) == 0
    x_spec = pl.BlockSpec((tm, d), lambda i: (i, 0))
    u_spec = pl.BlockSpec((tm, FFN_DIM), lambda i: (i, 0))
    return pl.pallas_call(
        functools.partial(_ffn_rows_kernel, tm=tm, tp=tp),
        grid=(rows // tm,),
        in_specs=[x_spec, u_spec, u_spec] + _ffn_weight_specs(w, layer),
        out_specs=[x_spec, u_spec],
        out_shape=[jax.ShapeDtypeStruct(x.shape, F32), jax.ShapeDtypeStruct((rows, FFN_DIM), F32)],
        compiler_params=_params(1),
        name="ffn_rows",
    )(x, prev1, prev2, *_ffn_weights(w))


def _kv_math(x, cos_ref, sin_ref, gsrc_ref, wdkv_ref, gkv_ref, wkr_ref, wkrot_ref,
             ckv_ref, kr_ref, ckvb_ref, krb_ref):
    hk = _rms(x, gsrc_ref[...]).astype(BF16)
    ckv = _rms(_dot(hk, wdkv_ref[...]), gkv_ref[...])
    kr = _dot(hk, wkr_ref[...]) * cos_ref[...] + _dot(hk, wkrot_ref[...]) * sin_ref[...]
    ckv_ref[...] = ckv
    kr_ref[...] = kr
    ckvb_ref[...] = ckv.astype(BF16)
    krb_ref[...] = kr.astype(BF16)


def _kv_kernel(x_ref, *refs):
    _kv_math(x_ref[...], *refs)


KV_KEYS = ("gsrc", "wdkv", "gkv", "wkr", "wkrot")


def _kv_specs(rows, d, period, tm):
    t_spec = pl.BlockSpec((tm, MLA_ROPE), lambda i: (i % (period // tm), 0))
    c_spec = pl.BlockSpec((tm, MLA_KV_RANK), lambda i: (i, 0))
    r_spec = pl.BlockSpec((tm, MLA_ROPE), lambda i: (i, 0))
    in_specs = [t_spec, t_spec, _const_spec((1, d)), _const_spec((d, MLA_KV_RANK)),
                _const_spec((1, MLA_KV_RANK)), _const_spec((d, MLA_ROPE)), _const_spec((d, MLA_ROPE))]
    out_shape = [jax.ShapeDtypeStruct((rows, MLA_KV_RANK), F32), jax.ShapeDtypeStruct((rows, MLA_ROPE), F32),
                 jax.ShapeDtypeStruct((rows, MLA_KV_RANK), BF16), jax.ShapeDtypeStruct((rows, MLA_ROPE), BF16)]
    return in_specs, [c_spec, r_spec, c_spec, r_spec], out_shape


def _kv_proj(x, cos, sin, w, *, tm):
    rows, d = x.shape
    period = cos.shape[0]
    assert rows % tm == 0 and period % tm == 0
    in_specs, out_specs, out_shape = _kv_specs(rows, d, period, tm)
    return pl.pallas_call(
        _kv_kernel,
        grid=(rows // tm,),
        in_specs=[pl.BlockSpec((tm, d), lambda i: (i, 0))] + in_specs,
        out_specs=out_specs,
        out_shape=out_shape,
        compiler_params=_params(1),
        name="kv_proj",
    )(x, cos, sin, *[w[k] for k in KV_KEYS])


def _mla_q_heads(x, cos, sin, gpre_ref, wdq_ref, qn_ref, wuq_ref, wuk_ref):
    h = _rms(x, gpre_ref[...]).astype(BF16)
    cq = _rms(_dot(h, wdq_ref[...]), qn_ref[...]).astype(BF16)
    q = _dot(cq, wuq_ref[...])
    hw = MLA_H * LANES
    for hd in range(MLA_H):
        qn = q[:, hd * MLA_NOPE:(hd + 1) * MLA_NOPE].astype(BF16)
        qa = _dot_nt(qn, wuk_ref[hd])
        roped = (q[:, hw + hd * LANES:hw + (hd + 1) * LANES] * cos
                 + q[:, 2 * hw + hd * LANES:2 * hw + (hd + 1) * LANES] * sin)
        yield hd, qa, roped[:, :MLA_ROPE]


def _lane_fold(x, op):
    parts = [x[:, c:c + LANES] for c in range(0, x.shape[1], LANES)]
    while len(parts) > 1:
        parts = [op(parts[i], parts[i + 1]) for i in range(0, len(parts) - 1, 2)] + parts[len(parts) & ~1:]
    return parts[0]


def _lane_tile(x, width):
    return x if width == LANES else jnp.concatenate([x] * (width // LANES), axis=1)


def _softmax_step(s, pv_fn, m_ref, l_ref, acc_ref, rows):
    m_prev = m_ref[rows, :]
    m_new = jnp.maximum(m_prev, jnp.max(_lane_fold(s, jnp.maximum), axis=1, keepdims=True))
    alpha = jnp.exp2(m_prev - m_new)
    p = jnp.exp2(s - _lane_tile(m_new, s.shape[1]))
    l_ref[rows, :] = alpha * l_ref[rows, :] + _lane_fold(p, jnp.add)
    acc_ref[rows, :] = _lane_tile(alpha, acc_ref.shape[1]) * acc_ref[rows, :] + pv_fn(p.astype(BF16))
    m_ref[rows, :] = m_new


def _softmax_result(l_ref, acc_ref, rows):
    return acc_ref[rows, :] / jnp.sum(l_ref[rows, :], axis=1, keepdims=True)


def _mla_out_heads(x, ol_fn, oc_ref, wuv_ref, wo_ref, gpost_ref):
    for hd in range(MLA_H):
        oc_ref[:, hd * MLA_V:(hd + 1) * MLA_V] = _dot(ol_fn(hd).astype(BF16), wuv_ref[hd])
    mix = _dot(oc_ref[...].astype(BF16), wo_ref[...])
    return x + _rms(mix, gpost_ref[...])


def _mla_prompt_kernel(x_ref, ckv_ref, kr_ref, cos_ref, sin_ref, gpre_ref, wdq_ref, qn_ref, wuq_ref,
                       wuk_ref, wuv_ref, wo_ref, gpost_ref, y_ref,
                       qa_ref, qr_ref, m_ref, l_ref, acc_ref, oc_ref, *, tq):
    i = pl.program_id(1)
    x = x_ref[...]
    for hd, qa, qr in _mla_q_heads(x, cos_ref[...], sin_ref[...], gpre_ref, wdq_ref, qn_ref, wuq_ref, wuk_ref):
        qa_ref[hd * tq:(hd + 1) * tq, :] = qa.astype(BF16)
        qr_ref[hd * tq:(hd + 1) * tq, :] = qr.astype(BF16)
    m_ref[...] = jnp.full(m_ref.shape, -jnp.inf, F32)
    l_ref[...] = jnp.zeros(l_ref.shape, F32)
    acc_ref[...] = jnp.zeros(acc_ref.shape, F32)

    gm = MLA_HEAD_GROUP * tq

    def key_block(k0, width, causal):
        kc = ckv_ref[pl.ds(k0, width), :]
        kk = kr_ref[pl.ds(k0, width), :]
        if causal:
            visible = (lax.broadcasted_iota(jnp.int32, (gm, width), 1)
                       <= (lax.broadcasted_iota(jnp.int32, (gm, width), 0) & (tq - 1)))
        for g in range(MLA_H // MLA_HEAD_GROUP):
            rows = slice(g * gm, (g + 1) * gm)
            s = (_dot_nt(qa_ref[rows, :], kc) + _dot_nt(qr_ref[rows, :], kk)) * SM_SCALE_LOG2
            if causal:
                s = jnp.where(visible, s, -jnp.inf)
            _softmax_step(s, lambda p: _dot(p, kc), m_ref, l_ref, acc_ref, rows)

    def body(j, carry):
        key_block(pl.multiple_of(j * tq, tq), tq, False)
        return carry

    lax.fori_loop(0, i, body, 0)
    key_block(pl.multiple_of(i * tq, tq), tq, True)

    def ol_fn(hd):
        return _softmax_result(l_ref, acc_ref, slice(hd * tq, (hd + 1) * tq))

    y_ref[...] = _mla_out_heads(x, ol_fn, oc_ref, wuv_ref, wo_ref, gpost_ref)


def _mla_weight_specs(w, keys):
    return [_const_spec(w[k].shape) for k in keys]


MLA_HEAD_GROUP = 8
MLA_Q_KEYS = ("gpre", "wdq", "qnorm", "wuq", "wuk")
MLA_O_KEYS = ("wuv", "wo", "gpost")


def _mla_prompt(x, ckv_b, kr_b, cos, sin, w, *, tq, seq):
    rows, d = x.shape
    nt = seq // tq
    assert rows % seq == 0 and seq % tq == 0
    m = MLA_H * tq
    x_spec = pl.BlockSpec((tq, d), lambda b, i: (b * nt + i, 0))
    t_spec = pl.BlockSpec((tq, LANES), lambda b, i: (i, 0))
    keys = MLA_Q_KEYS + MLA_O_KEYS
    return pl.pallas_call(
        functools.partial(_mla_prompt_kernel, tq=tq),
        grid=(rows // seq, nt),
        in_specs=[x_spec,
                  pl.BlockSpec((seq, MLA_KV_RANK), lambda b, i: (b, 0)),
                  pl.BlockSpec((seq, MLA_ROPE), lambda b, i: (b, 0)),
                  t_spec, t_spec] + _mla_weight_specs(w, keys),
        out_specs=x_spec,
        out_shape=jax.ShapeDtypeStruct(x.shape, F32),
        scratch_shapes=[pltpu.VMEM((m, MLA_KV_RANK), BF16), pltpu.VMEM((m, MLA_ROPE), BF16),
                        pltpu.VMEM((m, LANES), F32), pltpu.VMEM((m, LANES), F32), pltpu.VMEM((m, MLA_KV_RANK), F32),
                        pltpu.VMEM((tq, MLA_H * MLA_V), F32)],
        compiler_params=_params(2),
        name="mla_prompt",
    )(x, ckv_b, kr_b, cos, sin, *[w[k] for k in keys])


def _mla_q_kernel(x_ref, cos_ref, sin_ref, gpre_ref, wdq_ref, qn_ref, wuq_ref, wuk_ref, qa_ref, qr_ref):
    for hd, qa, qr in _mla_q_heads(x_ref[...], cos_ref[...], sin_ref[...], gpre_ref, wdq_ref, qn_ref,
                                   wuq_ref, wuk_ref):
        qa_ref[hd] = qa
        qr_ref[hd] = qr


def _mla_q(x, cos, sin, w, *, tm):
    rows, d = x.shape
    assert rows % tm == 0
    return pl.pallas_call(
        _mla_q_kernel,
        grid=(rows // tm,),
        in_specs=[pl.BlockSpec((tm, d), lambda i: (i, 0)), pl.BlockSpec((tm, LANES), lambda i: (i, 0)),
                  pl.BlockSpec((tm, LANES), lambda i: (i, 0))] + _mla_weight_specs(w, MLA_Q_KEYS),
        out_specs=[pl.BlockSpec((MLA_H, tm, MLA_KV_RANK), lambda i: (0, i, 0)),
                   pl.BlockSpec((MLA_H, tm, MLA_ROPE), lambda i: (0, i, 0))],
        out_shape=[jax.ShapeDtypeStruct((MLA_H, rows, MLA_KV_RANK), F32),
                   jax.ShapeDtypeStruct((MLA_H, rows, MLA_ROPE), F32)],
        compiler_params=_params(1),
        name="mla_q",
    )(x, cos, sin, *[w[k] for k in MLA_Q_KEYS])


def _mla_attend_sample_kernel(pt_ref, qa_ref, qr_ref, ckvn_ref, krn_ref, ckv_hbm, krt_hbm, ol_ref,
                              ckv_buf, krt_buf, sem, m_ref, l_ref, acc_ref, *, tq, pages):
    b = pl.program_id(0)
    j = pl.program_id(1)
    nj = pl.num_programs(1)
    last = pl.num_programs(0) * nj - 1
    step = b * nj + j
    ahead = PAGE_SLOTS - 1
    slot = lax.rem(step, PAGE_SLOTS)
    every = slice(None)
    n_streams = m_ref.shape[0]
    per_stream = pages // n_streams

    def page_copies(n):
        sl = lax.rem(n, PAGE_SLOTS)
        src = jnp.minimum(n, last)
        bb = src // nj
        jj = src - bb * nj
        out = []
        for i in range(pages):
            pid = pt_ref[bb, jj * pages + i]
            out.append(pltpu.make_async_copy(ckv_hbm.at[pid], ckv_buf.at[sl, i], sem.at[sl]))
            out.append(pltpu.make_async_copy(krt_hbm.at[pid], krt_buf.at[sl, i], sem.at[sl]))
        return out

    @pl.when(step == 0)
    def _():
        for n in range(ahead):
            for c in page_copies(n):
                c.start()

    @pl.when(j == 0)
    def _():
        m_ref[...] = jnp.full(m_ref.shape, -jnp.inf, F32)
        l_ref[...] = jnp.zeros(l_ref.shape, F32)
        acc_ref[...] = jnp.zeros(acc_ref.shape, F32)

    qa = qa_ref[0].astype(BF16)
    qr = qr_ref[0].astype(BF16)
    for c in page_copies(step):
        c.wait()
    for st in range(n_streams):
        ids = range(st * per_stream, (st + 1) * per_stream)
        kcs = [ckv_buf[slot, i].astype(BF16) for i in ids]
        s = jnp.concatenate([_dot_nt(qa, kc) + _dot(qr, krt_buf[slot, i].astype(BF16)) for i, kc in zip(ids, kcs)],
                            axis=1) * SM_SCALE_LOG2
        page = kcs[0].shape[0]

        def pv_pages(p, kcs=kcs):
            out = _dot(p[:, :page], kcs[0])
            for i in range(1, len(kcs)):
                out = out + _dot(p[:, i * page:(i + 1) * page], kcs[i])
            return out

        _softmax_step(s, pv_pages, m_ref.at[st], l_ref.at[st], acc_ref.at[st], every)

    for c in page_copies(step + ahead):
        c.start()

    @pl.when(step == last)
    def _():
        for n in range(1, ahead + 1):
            for c in page_copies(last + n):
                c.wait()

    @pl.when(j == nj - 1)
    def _():
        kc = ckvn_ref[0].astype(BF16)
        kk = krn_ref[0].astype(BF16)
        n_new = kc.shape[0]
        qpos = lax.broadcasted_iota(jnp.int32, (qa.shape[0], 1), 0) & (tq - 1)
        kpos = lax.broadcasted_iota(jnp.int32, (1, n_new), 1)
        sn = jnp.where(kpos <= qpos, (_dot_nt(qa, kc) + _dot_nt(qr, kk)) * SM_SCALE_LOG2, -jnp.inf)
        _softmax_step(sn, lambda p: _dot(p, kc), m_ref.at[0], l_ref.at[0], acc_ref.at[0], every)
        m_all = m_ref[0]
        for st in range(1, n_streams):
            m_all = jnp.maximum(m_all, m_ref[st])
        l_all = jnp.zeros(l_ref.shape[1:], F32)
        acc_all = jnp.zeros(acc_ref.shape[1:], F32)
        for st in range(n_streams):
            w_st = jnp.exp2(m_ref[st] - m_all)
            l_all = l_all + w_st * l_ref[st]
            acc_all = acc_all + _lane_tile(w_st, acc_all.shape[1]) * acc_ref[st]
        ol_ref[0] = acc_all / jnp.sum(l_all, axis=1, keepdims=True)


SAMPLE_SOFTMAX_STREAMS = 2
PAGE_SLOTS = 3


def _mla_attend_sample(qa, qr, ckv_new, kr_new, cache_ckv, cache_krt, page_table, *, tq, pages):
    bsz, n_pages = page_table.shape
    page = cache_ckv.shape[1]
    assert n_pages % pages == 0 and pages % SAMPLE_SOFTMAX_STREAMS == 0
    assert tq & (tq - 1) == 0 and ckv_new.shape[1] == page
    m = MLA_H * tq
    q_spec = pl.BlockSpec((1, m, MLA_KV_RANK), lambda b, j, pt: (b, 0, 0))
    r_spec = pl.BlockSpec((1, m, MLA_ROPE), lambda b, j, pt: (b, 0, 0))
    hbm_spec = pl.BlockSpec(memory_space=pl.ANY)
    grid_spec = pltpu.PrefetchScalarGridSpec(
        num_scalar_prefetch=1,
        grid=(bsz, n_pages // pages),
        in_specs=[q_spec, r_spec,
                  pl.BlockSpec((1, page, MLA_KV_RANK), lambda b, j, pt: (b, 0, 0)),
                  pl.BlockSpec((1, page, MLA_ROPE), lambda b, j, pt: (b, 0, 0)),
                  hbm_spec, hbm_spec],
        out_specs=q_spec,
        scratch_shapes=[pltpu.VMEM((PAGE_SLOTS, pages, page, MLA_KV_RANK), cache_ckv.dtype),
                        pltpu.VMEM((PAGE_SLOTS, pages, MLA_ROPE, page), cache_krt.dtype),
                        pltpu.SemaphoreType.DMA((PAGE_SLOTS,)),
                        pltpu.VMEM((SAMPLE_SOFTMAX_STREAMS, m, LANES), F32),
                        pltpu.VMEM((SAMPLE_SOFTMAX_STREAMS, m, LANES), F32),
                        pltpu.VMEM((SAMPLE_SOFTMAX_STREAMS, m, MLA_KV_RANK), F32)],
    )
    return pl.pallas_call(
        functools.partial(_mla_attend_sample_kernel, tq=tq, pages=pages),
        grid_spec=grid_spec,
        out_shape=jax.ShapeDtypeStruct(qa.shape, F32),
        compiler_params=_params(2),
        name="mla_attend_sample",
    )(page_table, qa, qr, ckv_new, kr_new, cache_ckv, cache_krt)


def _mla_out_kernel(x_ref, ol_ref, wuv_ref, wo_ref, gpost_ref, y_ref, oc_ref):
    y_ref[...] = _mla_out_heads(x_ref[...], lambda hd: ol_ref[hd], oc_ref, wuv_ref, wo_ref, gpost_ref)


def _mla_out(x, ol, w, *, tm):
    rows, d = x.shape
    assert rows % tm == 0
    x_spec = pl.BlockSpec((tm, d), lambda i: (i, 0))
    return pl.pallas_call(
        _mla_out_kernel,
        grid=(rows // tm,),
        in_specs=[x_spec, pl.BlockSpec((MLA_H, tm, MLA_KV_RANK), lambda i: (0, i, 0))]
        + _mla_weight_specs(w, MLA_O_KEYS),
        out_specs=x_spec,
        out_shape=jax.ShapeDtypeStruct(x.shape, F32),
        scratch_shapes=[pltpu.VMEM((tm, MLA_H * MLA_V), F32)],
        compiler_params=_params(1),
        name="mla_out",
    )(x, ol, *[w[k] for k in MLA_O_KEYS])


def _row(v):
    return v.reshape(1, -1).astype(F32)


def _rot_cols(w):
    half = w.shape[-1] // 2
    return jnp.concatenate([-w[..., half:], w[..., :half]], axis=-1)


def _prep_weights(p):
    gla, mla = [], []
    for l in range(N_GLA):
        rank = p["gla_w_gk1"].shape[-1]
        gla.append(dict(
            gpre=_row(p["norm_mix_pre"][l]), gpost=_row(p["norm_mix_post"][l]),
            wqkvr=jnp.concatenate([p["gla_w_q"][l], p["gla_w_k"][l], p["gla_w_v"][l], p["gla_w_r"][l]],
                                  axis=1).astype(BF16),
            wg1=jnp.pad(p["gla_w_gk1"][l], ((0, 0), (0, LANES - rank))).astype(BF16),
            wg2=jnp.pad(p["gla_w_gk2"][l], ((0, LANES - rank), (0, 0))).astype(BF16),
            bgk=_row(p["gla_b_gk"][l]), gnorm=_row(p["gla_norm"][l]), wo=p["gla_w_o"][l].astype(BF16)))
    ffn = dict(
        gpre=p["norm_ffn_pre"][:, None, :].astype(F32), gpost=p["norm_ffn_post"][:, None, :].astype(F32),
        wg=p["ffn_w_gate"].astype(BF16), wu=p["ffn_w_up"].astype(BF16),
        cw=p["ffn_conv_w"].astype(F32), cb=p["ffn_conv_b"][:, None, :].astype(F32),
        wd=p["ffn_w_down"].astype(BF16))
    wuk = jnp.transpose(p["mla_w_uk"], (1, 0, 2)).astype(BF16)
    wuv = jnp.transpose(p["mla_w_uv"], (1, 0, 2)).astype(BF16)
    for j in range(DEPTH - N_GLA):
        l = N_GLA + j
        wuq = p["mla_w_uq"][j].reshape(MLA_Q_RANK, MLA_H, MLA_NOPE + MLA_ROPE)
        nope = wuq[:, :, :MLA_NOPE].reshape(MLA_Q_RANK, MLA_H * MLA_NOPE)
        rope = wuq[:, :, MLA_NOPE:]
        lane_pad = ((0, 0), (0, 0), (0, LANES - MLA_ROPE))
        rope_p = jnp.pad(rope, lane_pad).reshape(MLA_Q_RANK, MLA_H * LANES)
        rot_p = jnp.pad(_rot_cols(rope), lane_pad).reshape(MLA_Q_RANK, MLA_H * LANES)
        mla.append(dict(
            gpre=_row(p["norm_mix_pre"][l]), gpost=_row(p["norm_mix_post"][l]),
            wdq=p["mla_w_dq"][j].astype(BF16), qnorm=_row(p["mla_q_norm"][j]),
            wuq=jnp.concatenate([nope, rope_p, rot_p], axis=1).astype(BF16),
            wuk=wuk, wuv=wuv, wo=p["mla_w_o"][j].astype(BF16)))
    kv = dict(gsrc=_row(p["mla_kv_src_norm"]), wdkv=p["mla_w_dkv"].astype(BF16), gkv=_row(p["mla_kv_norm"]),
              wkr=p["mla_w_kr"].astype(BF16), wkrot=_rot_cols(p["mla_w_kr"]).astype(BF16))
    return gla, ffn, mla, kv


def _rope_tables(pos):
    half = MLA_ROPE // 2
    inv = ROPE_THETA ** (-jnp.arange(half, dtype=F32) / half)
    ang = pos.astype(F32)[:, None] * inv[None, :]
    cos, sin = jnp.cos(ang), jnp.sin(ang)
    return jnp.concatenate([cos, cos], axis=-1), jnp.concatenate([sin, sin], axis=-1)


def _trunk_prompt(x, weights):
    gla_w, ffn_w, mla_w, kv_w = weights
    bsz, t, d = x.shape
    x = x.reshape(bsz * t, d)
    cos, sin = _rope_tables(jnp.arange(t, dtype=jnp.int32))
    cos2, sin2 = jnp.concatenate([cos, cos], -1), jnp.concatenate([sin, sin], -1)
    s0 = jnp.zeros((1, bsz, GLA_H, GLA_DK, GLA_DV), F32)
    buf0 = jnp.zeros((bsz, CONV_W - 1, FFN_DIM), F32)
    tm_gla, tm_ffn, tm_kv, tq = 512, 512, 1024, 256
    states, bufs = [], []
    ckv = kr = ckv_b = kr_b = None
    for l in range(DEPTH):
        if l < N_GLA:
            x, s = _gla_layer(x, s0, 0, gla_w[l], tm=tm_gla, cp=GLA_CHUNK, n_valid=GLA_CHUNK,
                              tiles_per_state=t // tm_gla)
            states.append(s)
        else:
            if l == N_GLA:
                ckv, kr, ckv_b, kr_b = _kv_proj(x, cos, sin, kv_w, tm=tm_kv)
            x = _mla_prompt(x, ckv_b, kr_b, cos2, sin2, mla_w[l - N_GLA], tq=tq, seq=t)
        x, b = _ffn_seq(x, buf0, ffn_w, l, tm=tm_ffn, tiles_per_seq=t // tm_ffn)
        bufs.append(b)
    return (x.reshape(bsz, t, d), jnp.stack(states), jnp.stack(bufs),
            ckv.reshape(bsz, t, -1), kr.reshape(bsz, t, -1))


def _heads_to_seq(q, bsz, t):
    h, _, n = q.shape
    return q.reshape(h, bsz, -1, n)[:, :, :t].transpose(1, 0, 2, 3).reshape(bsz, h * t, n)


def _seq_to_heads(o, tp):
    bsz, m, n = o.shape
    t = m // MLA_H
    o = o.reshape(bsz, MLA_H, t, n).transpose(1, 0, 2, 3)
    return jnp.pad(o, ((0, 0), (0, 0), (0, tp - t), (0, 0))).reshape(MLA_H, bsz * tp, n)


def _trunk_sample(x, state_gla, state_conv, cache_ckv, cache_kr, page_table, weights):
    gla_w, ffn_w, mla_w, kv_w = weights
    bsz, t, d = x.shape
    tp = -(-t // SUBLANES) * SUBLANES
    assert tp & (tp - 1) == 0 and t >= CONV_W - 1
    rows = bsz * tp
    page = cache_ckv.shape[1]
    past = page_table.shape[1] * page
    x = jnp.pad(x, ((0, 0), (0, tp - t), (0, 0))).reshape(rows, d)
    cos, sin = _rope_tables(past + jnp.arange(tp, dtype=jnp.int32))
    cos, sin = jnp.tile(cos, (bsz, 1)), jnp.tile(sin, (bsz, 1))
    cos2, sin2 = jnp.concatenate([cos, cos], -1), jnp.concatenate([sin, sin], -1)
    cache_krt = jnp.swapaxes(cache_kr, 1, 2)
    tm_gla = 8 * tp
    pad_tail = ((0, 0), (0, tp - (CONV_W - 1)), (0, 0))
    states, bufs = [], []
    ckv = kr = ckv_pad = kr_pad = None
    for l in range(DEPTH):
        if l < N_GLA:
            x, s = _gla_layer(x, state_gla, l, gla_w[l], tm=tm_gla, cp=tp, n_valid=t, tiles_per_state=0)
            states.append(s)
        else:
            w = mla_w[l - N_GLA]
            if l == N_GLA:
                ckv, kr, _, _ = _kv_proj(x, cos, sin, kv_w, tm=rows)
                ckv = ckv.reshape(bsz, tp, -1)
                kr = kr.reshape(bsz, tp, -1)
                ckv_pad = jnp.pad(ckv, ((0, 0), (0, page - tp), (0, 0)))
                kr_pad = jnp.pad(kr, ((0, 0), (0, page - tp), (0, 0)))
            qa, qr = _mla_q(x, cos2, sin2, w, tm=rows)
            ol = _mla_attend_sample(_heads_to_seq(qa, bsz, t), _heads_to_seq(qr, bsz, t), ckv_pad, kr_pad,
                                    cache_ckv, cache_krt, page_table, tq=t, pages=32)
            x = _mla_out(x, _seq_to_heads(ol, tp), w, tm=rows)
        buf = state_conv[l]
        prev1 = jnp.pad(buf[:, CONV_W - 2:], ((0, 0), (0, tp - 1), (0, 0))).reshape(rows, FFN_DIM)
        prev2 = jnp.pad(buf, pad_tail).reshape(rows, FFN_DIM)
        x, u = _ffn_rows(x, prev1, prev2, ffn_w, l, tm=rows, tp=tp)
        bufs.append(u.reshape(bsz, tp, FFN_DIM)[:, t - (CONV_W - 1):t])
    return (x.reshape(bsz, tp, d)[:, :t], jnp.stack(states), jnp.stack(bufs), ckv[:, :t], kr[:, :t])


def kernel(x_prompt, x_sample, state_gla, state_ffn_conv, cache_ckv, cache_krope, page_table, norm_mix_pre, norm_mix_post, norm_ffn_pre, norm_ffn_post, ffn_w_gate, ffn_w_up, ffn_conv_w, ffn_conv_b, ffn_w_down, gla_w_q, gla_w_k, gla_w_v, gla_w_gk1, gla_w_gk2, gla_b_gk, gla_w_r, gla_norm, gla_w_o, mla_kv_src_norm, mla_w_dkv, mla_kv_norm, mla_w_kr, mla_w_uk, mla_w_uv, mla_w_dq, mla_q_norm, mla_w_uq, mla_w_o):
    p = dict(norm_mix_pre=norm_mix_pre, norm_mix_post=norm_mix_post, norm_ffn_pre=norm_ffn_pre,
             norm_ffn_post=norm_ffn_post, ffn_w_gate=ffn_w_gate, ffn_w_up=ffn_w_up, ffn_conv_w=ffn_conv_w,
             ffn_conv_b=ffn_conv_b, ffn_w_down=ffn_w_down, gla_w_q=gla_w_q, gla_w_k=gla_w_k, gla_w_v=gla_w_v,
             gla_w_gk1=gla_w_gk1, gla_w_gk2=gla_w_gk2, gla_b_gk=gla_b_gk, gla_w_r=gla_w_r, gla_norm=gla_norm,
             gla_w_o=gla_w_o, mla_kv_src_norm=mla_kv_src_norm, mla_w_dkv=mla_w_dkv, mla_kv_norm=mla_kv_norm,
             mla_w_kr=mla_w_kr, mla_w_uk=mla_w_uk, mla_w_uv=mla_w_uv, mla_w_dq=mla_w_dq, mla_q_norm=mla_q_norm,
             mla_w_uq=mla_w_uq, mla_w_o=mla_w_o)
    weights = _prep_weights(p)
    y_p, gla_p, conv_p, ckv_p, kr_p = _trunk_prompt(x_prompt, weights)
    y_s, gla_s, conv_s, ckv_s, kr_s = _trunk_sample(x_sample, state_gla, state_ffn_conv, cache_ckv,
                                                    cache_krope, page_table, weights)
    return (y_p, y_s, gla_p, gla_s, conv_p, conv_s, ckv_p, ckv_s, kr_p, kr_s)
```

```python
import functools

import jax
import jax.numpy as jnp
from jax import lax
from jax.experimental import pallas as pl
from jax.experimental.pallas import tpu as pltpu

D_MODEL = 1024
DEPTH = 4
N_GLA = DEPTH // 2
GLA_H = 4
GLA_DK = 128
GLA_DV = 256
GLA_GATE_TEMP = 16.0
GLA_CHUNK = 64
GLA_SUBTILE = 256
MLA_H = 8
MLA_NOPE = 128
MLA_ROPE = 64
MLA_V = 128
MLA_Q_RANK = 384
MLA_KV_RANK = 256
ROPE_THETA = 10000.0
FFN_DIM = 2816
CONV_W = 3
EPS = 1e-6
LANES = 128
SUBLANES = 8
MXU_DIM = 256
VMEM_LIMIT = 56 * 1024 * 1024

F32 = jnp.float32
BF16 = jnp.bfloat16
NT_DIMS = (((1,), (1,)), ((), ()))
TN_DIMS = (((0,), (0,)), ((), ()))
LOG2_E = 1.4426950408889634
SM_SCALE_LOG2 = (MLA_NOPE + MLA_ROPE) ** -0.5 * LOG2_E


def _rms(x, g):
    return x * lax.rsqrt(jnp.mean(x * x, axis=-1, keepdims=True) + EPS) * g


def _dot(a, b):
    return jnp.dot(a, b, preferred_element_type=F32)


def _dot_nt(a, b):
    return lax.dot_general(a, b, NT_DIMS, preferred_element_type=F32)


def _log_sigmoid(z):
    return jnp.minimum(z, 0.0) - jnp.log1p(jnp.exp(-jnp.abs(z)))


def _gelu_tanh(x):
    cdf = 0.5 * (1.0 + jnp.tanh(0.7978845608028654 * (x + 0.044715 * (x * x * x))))
    return x * cdf


def _const_spec(shape):
    nd = len(shape)
    return pl.BlockSpec(shape, lambda *_: (0,) * nd, pipeline_mode=pl.Buffered(1))


def _params(n_axes):
    return pltpu.CompilerParams(dimension_semantics=("arbitrary",) * n_axes,
                                vmem_limit_bytes=VMEM_LIMIT)


def _gla_kernel(x_ref, s0_ref, gpre_ref, wqkvr_ref, wg1_ref, wg2_ref, bgk_ref, gn_ref, wo_ref,
                gpost_ref, y_ref, s_ref, proj_ref, lg_ref, o_ref, *, tm, cp, n_valid, tiles_per_state):
    if tiles_per_state:
        @pl.when(pl.program_id(0) % tiles_per_state == 0)
        def _():
            s_ref[...] = s0_ref[...]
    else:
        s_ref[...] = s0_ref[...]

    x = x_ref[...]
    h = _rms(x, gpre_ref[...]).astype(BF16)
    g1 = _dot(h, wg1_ref[...])
    z = _dot(g1.astype(BF16), wg2_ref[...]) + bgk_ref[...]
    lg = _log_sigmoid(z) / GLA_GATE_TEMP
    if n_valid < cp:
        t_in_chunk = lax.broadcasted_iota(jnp.int32, (tm, 1), 0) & (cp - 1)
        lg = jnp.where(t_in_chunk < n_valid, lg, 0.0)
    lg_ref[...] = lg
    proj_ref[...] = _dot(h, wqkvr_ref[...])

    ts = min(tm, GLA_SUBTILE)
    nc = ts // cp
    r_i = lax.broadcasted_iota(jnp.int32, (ts, ts), 0)
    c_i = lax.broadcasted_iota(jnp.int32, (ts, ts), 1)
    intra = (r_i >= c_i) & ((r_i & -cp) == (c_i & -cp))
    t_in = lax.broadcasted_iota(jnp.int32, (ts, GLA_DK), 0) & (cp - 1)
    eye =(lax.broadcasted_iota(jnp.int32, (GLA_DK, GLA_DK), 0)
           == lax.broadcasted_iota(jnp.int32, (GLA_DK, GLA_DK), 1))
    k_off = GLA_H * GLA_DK
    v_off = 2 * GLA_H * GLA_DK
    r_off = v_off + GLA_H * GLA_DV

    for hd in range(GLA_H):
        kcol = slice(hd * GLA_DK, (hd + 1) * GLA_DK)
        s = s_ref[0, hd]
        for sub in range(tm // ts):
            r0 = sub * ts
            b = lg_ref[r0:r0 + ts, kcol]
            shift = 1
            while shift < cp:
                b = b + jnp.where(t_in >= shift, pltpu.roll(b, shift, 0), 0.0)
                shift *= 2
            bl = jnp.concatenate(
                [jnp.broadcast_to(b[(c + 1) * cp - 1:(c + 1) * cp, :], (cp, GLA_DK)) for c in range(nc)], axis=0)
            q = proj_ref[r0:r0 + ts, hd * GLA_DK:(hd + 1) * GLA_DK] * (GLA_DK ** -0.5)
            k = proj_ref[r0:r0 + ts, k_off + hd * GLA_DK:k_off + (hd + 1) * GLA_DK]
            v = proj_ref[r0:r0 + ts, v_off + hd * GLA_DV:v_off + (hd + 1) * GLA_DV]
            qe = q * jnp.exp(b)
            ke = k * jnp.exp(-b)
            kd = k * jnp.exp(bl - b)
            a = jnp.where(intra, _dot_nt(qe.astype(BF16), ke.astype(BF16)), 0.0).astype(BF16)
            o_intra = _dot(a, v.astype(BF16))
            for c in range(nc):
                rows = slice(c * cp, (c + 1) * cp)
                si = sub * nc + c
                if not tiles_per_state:
                    s = s_ref[si, hd]
                o_ref[r0 + c * cp:r0 + (c + 1) * cp, hd * GLA_DV:(hd + 1) * GLA_DV] = (
                    o_intra[rows, :] + _dot(qe[rows, :].astype(BF16), s.astype(BF16)))
                dcol = jnp.sum(jnp.where(eye, jnp.exp(bl[c * cp:c * cp + 1, :]), 0.0), axis=1, keepdims=True)
                s = dcol * s + lax.dot_general(kd[rows, :].astype(BF16), v[rows, :].astype(BF16), TN_DIMS,
                                               preferred_element_type=F32)
                if not tiles_per_state:
                    s_ref[si, hd] = s
        if tiles_per_state:
            s_ref[0, hd] = s

    gn = gn_ref[...]
    for hd in range(GLA_H):
        vcol = slice(hd * GLA_DV, (hd + 1) * GLA_DV)
        r = proj_ref[:, r_off + hd * GLA_DV:r_off + (hd + 1) * GLA_DV]
        o_ref[:, vcol] = _rms(o_ref[:, vcol], gn) * (r * jax.nn.sigmoid(r))
    mix = _dot(o_ref[...].astype(BF16), wo_ref[...])
    y_ref[...] = x + _rms(mix, gpost_ref[...])


def _gla_layer(x, s0_all, layer, w, *, tm, cp, n_valid, tiles_per_state):
    rows, d = x.shape
    assert rows % tm == 0 and tm % cp == 0 and cp & (cp - 1) == 0
    n_proj = w["wqkvr"].shape[1]
    x_spec = pl.BlockSpec((tm, d), lambda i: (i, 0))
    n_blk = 1 if tiles_per_state else tm // cp
    per = tiles_per_state or 1
    s_in_spec = pl.BlockSpec((None, n_blk, GLA_H, GLA_DK, GLA_DV), lambda i: (layer, i // per, 0, 0, 0))
    s_spec = pl.BlockSpec((n_blk, GLA_H, GLA_DK, GLA_DV), lambda i: (i // per, 0, 0, 0))
    return pl.pallas_call(
        functools.partial(_gla_kernel, tm=tm, cp=cp, n_valid=n_valid, tiles_per_state=tiles_per_state),
        grid=(rows // tm,),
        in_specs=[x_spec, s_in_spec, _const_spec((1, d)), _const_spec(w["wqkvr"].shape),
                  _const_spec(w["wg1"].shape), _const_spec(w["wg2"].shape), _const_spec(w["bgk"].shape),
                  _const_spec(w["gnorm"].shape), _const_spec(w["wo"].shape), _const_spec((1, d))],
        out_specs=[x_spec, s_spec],
        out_shape=[jax.ShapeDtypeStruct(x.shape, F32), jax.ShapeDtypeStruct(s0_all.shape[1:], F32)],
        scratch_shapes=[pltpu.VMEM((tm, n_proj), F32), pltpu.VMEM((tm, GLA_H * GLA_DK), F32),
                        pltpu.VMEM((tm, GLA_H * GLA_DV), F32)],
        compiler_params=_params(1),
        name="gla_layer",
    )(x, s0_all, w["gpre"], w["wqkvr"], w["wg1"], w["wg2"], w["bgk"], w["gnorm"], w["wo"], w["gpost"])


def _ffn_core(x, gpre_ref, wg_ref, wu_ref, cw_ref, cb_ref, wd_ref, gpost_ref, shifted, emit_u):
    h = _rms(x, gpre_ref[...]).astype(BF16)
    u = _dot(h, wg_ref[...])
    up = _dot(h, wu_ref[...])
    u1, u2 = shifted(u)
    c = cb_ref[...] + ((cw_ref[0:1, :] * u2 + cw_ref[1:2, :] * u1) + cw_ref[2:3, :] * u)
    act = (_gelu_tanh(c) * up).astype(BF16)
    emit_u(u)
    return x + _rms(_dot(act, wd_ref[...]), gpost_ref[...])


def _ffn_seq_kernel(x_ref, buf_ref, gpre_ref, wg_ref, wu_ref, cw_ref, cb_ref, wd_ref, gpost_ref,
                    y_ref, bufn_ref, carry_ref, *, tm, tiles_per_seq):
    @pl.when(pl.program_id(0) % tiles_per_seq == 0)
    def _():
        carry_ref[...] = buf_ref[0]

    row = lax.broadcasted_iota(jnp.int32, (tm, 1), 0)

    def shifted(u):
        p0 = carry_ref[0:1, :]
        p1 = carry_ref[1:2, :]
        u1 = jnp.where(row == 0, p1, pltpu.roll(u, 1, 0))
        u2 = jnp.where(row == 0, p0, jnp.where(row == 1, p1, pltpu.roll(u, 2, 0)))
        return u1, u2

    def emit_u(u):
        tail = u[tm - (CONV_W - 1):tm, :]
        carry_ref[...] = tail
        bufn_ref[0] = tail

    y_ref[...] = _ffn_core(x_ref[...], gpre_ref, wg_ref, wu_ref, cw_ref, cb_ref, wd_ref, gpost_ref,
                           shifted, emit_u)


def _ffn_rows_kernel(x_ref, prev1_ref, prev2_ref, gpre_ref, wg_ref, wu_ref, cw_ref, cb_ref, wd_ref,
                     gpost_ref, y_ref, u_ref, *, tm, tp):
    t = lax.broadcasted_iota(jnp.int32, (tm, 1), 0) & (tp - 1)

    def shifted(u):
        u1 = jnp.where(t == 0, prev1_ref[...], pltpu.roll(u, 1, 0))
        u2 = jnp.where(t < 2, prev2_ref[...], pltpu.roll(u, 2, 0))
        return u1, u2

    def emit_u(u):
        u_ref[...] = u

    y_ref[...] = _ffn_core(x_ref[...], gpre_ref, wg_ref, wu_ref, cw_ref, cb_ref, wd_ref, gpost_ref,
                           shifted, emit_u)


FFN_KEYS = ("gpre", "wg", "wu", "cw", "cb", "wd", "gpost")


def _ffn_weight_specs(w, layer):
    return [pl.BlockSpec((None,) + w[k].shape[1:], lambda *_: (layer, 0, 0), pipeline_mode=pl.Buffered(1))
            for k in FFN_KEYS]


def _ffn_weights(w):
    return [w[k] for k in FFN_KEYS]


def _ffn_seq(x, buf, w, layer, *, tm, tiles_per_seq):
    rows, d = x.shape
    assert rows % (tm * tiles_per_seq) == 0 and tm >= CONV_W - 1
    x_spec = pl.BlockSpec((tm, d), lambda i: (i, 0))
    b_spec = pl.BlockSpec((1, CONV_W - 1, FFN_DIM), lambda i: (i // tiles_per_seq, 0, 0))
    return pl.pallas_call(
        functools.partial(_ffn_seq_kernel, tm=tm, tiles_per_seq=tiles_per_seq),
        grid=(rows // tm,),
        in_specs=[x_spec, b_spec] + _ffn_weight_specs(w, layer),
        out_specs=[x_spec, b_spec],
        out_shape=[jax.ShapeDtypeStruct(x.shape, F32), jax.ShapeDtypeStruct(buf.shape, F32)],
        scratch_shapes=[pltpu.VMEM((CONV_W - 1, FFN_DIM), F32)],
        compiler_params=_params(1),
        name="ffn_seq",
    )(x, buf, *_ffn_weights(w))


def _ffn_rows(x, prev1, prev2, w, layer, *, tm, tp):
    rows, d = x.shape
    assert rows % tm == 0 and tm % tp == 0 and tp & (tp - 1) == 0
    x_spec = pl.BlockSpec((tm, d), lambda i: (i, 0))
    u_spec = pl.BlockSpec((tm, FFN_DIM), lambda i: (i, 0))
    return pl.pallas_call(
        functools.partial(_ffn_rows_kernel, tm=tm, tp=tp),
        grid=(rows // tm,),
        in_specs=[x_spec, u_spec, u_spec] + _ffn_weight_specs(w, layer),
        out_specs=[x_spec, u_spec],
        out_shape=[jax.ShapeDtypeStruct(x.shape, F32), jax.ShapeDtypeStruct((rows, FFN_DIM), F32)],
        compiler_params=_params(1),
        name="ffn_rows",
    )(x, prev1, prev2, *_ffn_weights(w))


def _kv_math(x, cos_ref, sin_ref, gsrc_ref, wdkv_ref, gkv_ref, wkr_ref, wkrot_ref,
             ckv_ref, kr_ref, ckvb_ref, krb_ref):
    hk = _rms(x, gsrc_ref[...]).astype(BF16)
    ckv = _rms(_dot(hk, wdkv_ref[...]), gkv_ref[...])
    kr = _dot(hk, wkr_ref[...]) * cos_ref[...] + _dot(hk, wkrot_ref[...]) * sin_ref[...]
    ckv_ref[...] = ckv
    kr_ref[...] = kr
    ckvb_ref[...] = ckv.astype(BF16)
    krb_ref[...] = kr.astype(BF16)


def _kv_kernel(x_ref, *refs):
    _kv_math(x_ref[...], *refs)


KV_KEYS = ("gsrc", "wdkv", "gkv", "wkr", "wkrot")


def _kv_specs(rows, d, period, tm):
    t_spec = pl.BlockSpec((tm, MLA_ROPE), lambda i: (i % (period // tm), 0))
    c_spec = pl.BlockSpec((tm, MLA_KV_RANK), lambda i: (i, 0))
    r_spec = pl.BlockSpec((tm, MLA_ROPE), lambda i: (i, 0))
    in_specs = [t_spec, t_spec, _const_spec((1, d)), _const_spec((d, MLA_KV_RANK)),
                _const_spec((1, MLA_KV_RANK)), _const_spec((d, MLA_ROPE)), _const_spec((d, MLA_ROPE))]
    out_shape = [jax.ShapeDtypeStruct((rows, MLA_KV_RANK), F32), jax.ShapeDtypeStruct((rows, MLA_ROPE), F32),
                 jax.ShapeDtypeStruct((rows, MLA_KV_RANK), BF16), jax.ShapeDtypeStruct((rows, MLA_ROPE), BF16)]
    return in_specs, [c_spec, r_spec, c_spec, r_spec], out_shape


def _kv_proj(x, cos, sin, w, *, tm):
    rows, d = x.shape
    period = cos.shape[0]
    assert rows % tm == 0 and period % tm == 0
    in_specs, out_specs, out_shape = _kv_specs(rows, d, period, tm)
    return pl.pallas_call(
        _kv_kernel,
        grid=(rows // tm,),
        in_specs=[pl.BlockSpec((tm, d), lambda i: (i, 0))] + in_specs,
        out_specs=out_specs,
        out_shape=out_shape,
        compiler_params=_params(1),
        name="kv_proj",
    )(x, cos, sin, *[w[k] for k in KV_KEYS])


def _mla_q_heads(x, cos, sin, gpre_ref, wdq_ref, qn_ref, wuq_ref, wuk_ref):
    h = _rms(x, gpre_ref[...]).astype(BF16)
    cq = _rms(_dot(h, wdq_ref[...]), qn_ref[...]).astype(BF16)
    q = _dot(cq, wuq_ref[...])
    hw = MLA_H * LANES
    for hd in range(MLA_H):
        qn = q[:, hd * MLA_NOPE:(hd + 1) * MLA_NOPE].astype(BF16)
        qa = _dot_nt(qn, wuk_ref[hd])
        roped = (q[:, hw + hd * LANES:hw + (hd + 1) * LANES] * cos
                 + q[:, 2 * hw + hd * LANES:2 * hw + (hd + 1) * LANES] * sin)
        yield hd, qa, roped[:, :MLA_ROPE]


def _lane_fold(x, op):
    parts = [x[:, c:c + LANES] for c in range(0, x.shape[1], LANES)]
    while len(parts) > 1:
        parts = [op(parts[i], parts[i + 1]) for i in range(0, len(parts) - 1, 2)] + parts[len(parts) & ~1:]
    return parts[0]


def _lane_tile(x, width):
    return x if width == LANES else jnp.concatenate([x] * (width // LANES), axis=1)


def _softmax_step(s, pv_fn, m_ref, l_ref, acc_ref, rows):
    m_prev = m_ref[rows, :]
    m_new = jnp.maximum(m_prev, jnp.max(_lane_fold(s, jnp.maximum), axis=1, keepdims=True))
    alpha = jnp.exp2(m_prev - m_new)
    p = jnp.exp2(s - _lane_tile(m_new, s.shape[1]))
    l_ref[rows, :] = alpha * l_ref[rows, :] + _lane_fold(p, jnp.add)
    acc_ref[rows, :] = _lane_tile(alpha, acc_ref.shape[1]) * acc_ref[rows, :] + pv_fn(p.astype(BF16))
    m_ref[rows, :] = m_new


def _softmax_result(l_ref, acc_ref, rows):
    return acc_ref[rows, :] / jnp.sum(l_ref[rows, :], axis=1, keepdims=True)


def _mla_out_heads(x, ol_fn, oc_ref, wuv_ref, wo_ref, gpost_ref):
    for hd in range(MLA_H):
        oc_ref[:, hd * MLA_V:(hd + 1) * MLA_V] = _dot(ol_fn(hd).astype(BF16), wuv_ref[hd])
    mix = _dot(oc_ref[...].astype(BF16), wo_ref[...])
    return x + _rms(mix, gpost_ref[...])


def _mla_prompt_kernel(x_ref, ckv_ref, kr_ref, cos_ref, sin_ref, gpre_ref, wdq_ref, qn_ref, wuq_ref,
                       wuk_ref, wuv_ref, wo_ref, gpost_ref, y_ref,
                       qa_ref, qr_ref, m_ref, l_ref, acc_ref, oc_ref, *, tq):
    i = pl.program_id(1)
    x = x_ref[...]
    for hd, qa, qr in _mla_q_heads(x, cos_ref[...], sin_ref[...], gpre_ref, wdq_ref, qn_ref, wuq_ref, wuk_ref):
        qa_ref[hd * tq:(hd + 1) * tq, :] = (qa * SM_SCALE_LOG2).astype(BF16)
        qr_ref[hd * tq:(hd + 1) * tq, :] = (qr * SM_SCALE_LOG2).astype(BF16)
    m_ref[...] = jnp.full(m_ref.shape, -jnp.inf, F32)
    l_ref[...] = jnp.zeros(l_ref.shape, F32)
    acc_ref[...] = jnp.zeros(acc_ref.shape, F32)

    every = slice(None)

    def key_block(j, causal):
        k0 = pl.multiple_of(j * tq, tq)
        kc = ckv_ref[pl.ds(k0, tq), :]
        kk = kr_ref[pl.ds(k0, tq), :]
        s = _dot_nt(qa_ref[...], kc) + _dot_nt(qr_ref[...], kk)
        if causal:
            visible = (lax.broadcasted_iota(jnp.int32, s.shape, 1)
                       <= (lax.broadcasted_iota(jnp.int32, s.shape, 0) & (tq - 1)))
            s = jnp.where(visible, s, -jnp.inf)
        _softmax_step(s, lambda p: _dot(p, kc), m_ref, l_ref, acc_ref, every)

    def body(j, carry):
        key_block(j, False)
        return carry

    lax.fori_loop(0, i, body, 0)
    key_block(i, True)

    def ol_fn(hd):
        return _softmax_result(l_ref, acc_ref, slice(hd * tq, (hd + 1) * tq))

    y_ref[...] = _mla_out_heads(x, ol_fn, oc_ref, wuv_ref, wo_ref, gpost_ref)


def _mla_weight_specs(w, keys):
    return [_const_spec(w[k].shape) for k in keys]


MLA_Q_KEYS = ("gpre", "wdq", "qnorm", "wuq", "wuk")
MLA_O_KEYS = ("wuv", "wo", "gpost")


def _mla_prompt(x, ckv_b, kr_b, cos, sin, w, *, tq, seq):
    rows, d = x.shape
    nt = seq // tq
    assert rows % seq == 0 and seq % tq == 0
    m = MLA_H * tq
    x_spec = pl.BlockSpec((tq, d), lambda b, i: (b * nt + i, 0))
    t_spec = pl.BlockSpec((tq, LANES), lambda b, i: (i, 0))
    keys = MLA_Q_KEYS + MLA_O_KEYS
    return pl.pallas_call(
        functools.partial(_mla_prompt_kernel, tq=tq),
        grid=(rows // seq, nt),
        in_specs=[x_spec,
                  pl.BlockSpec((seq, MLA_KV_RANK), lambda b, i: (b, 0)),
                  pl.BlockSpec((seq, MLA_ROPE), lambda b, i: (b, 0)),
                  t_spec, t_spec] + _mla_weight_specs(w, keys),
        out_specs=x_spec,
        out_shape=jax.ShapeDtypeStruct(x.shape, F32),
        scratch_shapes=[pltpu.VMEM((m, MLA_KV_RANK), BF16), pltpu.VMEM((m, MLA_ROPE), BF16),
                        pltpu.VMEM((m, LANES), F32), pltpu.VMEM((m, LANES), F32), pltpu.VMEM((m, MLA_KV_RANK), F32),
                        pltpu.VMEM((tq, MLA_H * MLA_V), F32)],
        compiler_params=_params(2),
        name="mla_prompt",
    )(x, ckv_b, kr_b, cos, sin, *[w[k] for k in keys])


def _mla_q_kernel(x_ref, cos_ref, sin_ref, gpre_ref, wdq_ref, qn_ref, wuq_ref, wuk_ref, qa_ref, qr_ref):
    for hd, qa, qr in _mla_q_heads(x_ref[...], cos_ref[...], sin_ref[...], gpre_ref, wdq_ref, qn_ref,
                                   wuq_ref, wuk_ref):
        qa_ref[hd] = qa
        qr_ref[hd] = qr


def _mla_q(x, cos, sin, w, *, tm):
    rows, d = x.shape
    assert rows % tm == 0
    return pl.pallas_call(
        _mla_q_kernel,
        grid=(rows // tm,),
        in_specs=[pl.BlockSpec((tm, d), lambda i: (i, 0)), pl.BlockSpec((tm, LANES), lambda i: (i, 0)),
                  pl.BlockSpec((tm, LANES), lambda i: (i, 0))] + _mla_weight_specs(w, MLA_Q_KEYS),
        out_specs=[pl.BlockSpec((MLA_H, tm, MLA_KV_RANK), lambda i: (0, i, 0)),
                   pl.BlockSpec((MLA_H, tm, MLA_ROPE), lambda i: (0, i, 0))],
        out_shape=[jax.ShapeDtypeStruct((MLA_H, rows, MLA_KV_RANK), F32),
                   jax.ShapeDtypeStruct((MLA_H, rows, MLA_ROPE), F32)],
        compiler_params=_params(1),
        name="mla_q",
    )(x, cos, sin, *[w[k] for k in MLA_Q_KEYS])


def _mla_attend_sample_kernel(pt_ref, qa_ref, qr_ref, ckvn_ref, krn_ref, ckv_hbm, krt_hbm, ol_ref,
                              ckv_buf, krt_buf, sem, m_ref, l_ref, acc_ref, *, tq, pages):
    b = pl.program_id(0)
    j = pl.program_id(1)
    nj = pl.num_programs(1)
    last = pl.num_programs(0) * nj - 1
    step = b * nj + j
    ahead = PAGE_SLOTS - 1
    slot = lax.rem(step, PAGE_SLOTS)
    every = slice(None)
    n_streams = m_ref.shape[0]
    per_stream = pages // n_streams

    def page_copies(n):
        sl = lax.rem(n, PAGE_SLOTS)
        src = jnp.minimum(n, last)
        bb = src // nj
        jj = src - bb * nj
        out = []
        for i in range(pages):
            pid = pt_ref[bb, jj * pages + i]
            out.append(pltpu.make_async_copy(ckv_hbm.at[pid], ckv_buf.at[sl, i], sem.at[sl]))
            out.append(pltpu.make_async_copy(krt_hbm.at[pid], krt_buf.at[sl, i], sem.at[sl]))
        return out

    @pl.when(step == 0)
    def _():
        for n in range(ahead):
            for c in page_copies(n):
                c.start()

    @pl.when(j == 0)
    def _():
        m_ref[...] = jnp.full(m_ref.shape, -jnp.inf, F32)
        l_ref[...] = jnp.zeros(l_ref.shape, F32)
        acc_ref[...] = jnp.zeros(acc_ref.shape, F32)

    qa = qa_ref[0].astype(BF16)
    qr = qr_ref[0].astype(BF16)
    for c in page_copies(step):
        c.wait()
    for st in range(n_streams):
        ids = range(st * per_stream, (st + 1) * per_stream)
        kcs = [ckv_buf[slot, i].astype(BF16) for i in ids]
        s = jnp.concatenate([_dot_nt(qa, kc) + _dot(qr, krt_buf[slot, i].astype(BF16)) for i, kc in zip(ids, kcs)],
                            axis=1) * SM_SCALE_LOG2
        page = kcs[0].shape[0]

        def pv_pages(p, kcs=kcs):
            out = _dot(p[:, :page], kcs[0])
            for i in range(1, len(kcs)):
                out = out + _dot(p[:, i * page:(i + 1) * page], kcs[i])
            return out

        _softmax_step(s, pv_pages, m_ref.at[st], l_ref.at[st], acc_ref.at[st], every)

    for c in page_copies(step + ahead):
        c.start()

    @pl.when(step == last)
    def _():
        for n in range(1, ahead + 1):
            for c in page_copies(last + n):
                c.wait()

    @pl.when(j == nj - 1)
    def _():
        kc = ckvn_ref[0].astype(BF16)
        kk = krn_ref[0].astype(BF16)
        n_new = kc.shape[0]
        qpos = lax.broadcasted_iota(jnp.int32, (qa.shape[0], 1), 0) & (tq - 1)
        kpos = lax.broadcasted_iota(jnp.int32, (1, n_new), 1)
        sn = jnp.where(kpos <= qpos, (_dot_nt(qa, kc) + _dot_nt(qr, kk)) * SM_SCALE_LOG2, -jnp.inf)
        _softmax_step(sn, lambda p: _dot(p, kc), m_ref.at[0], l_ref.at[0], acc_ref.at[0], every)
        m_all = m_ref[0]
        for st in range(1, n_streams):
            m_all = jnp.maximum(m_all, m_ref[st])
        l_all = jnp.zeros(l_ref.shape[1:], F32)
        acc_all = jnp.zeros(acc_ref.shape[1:], F32)
        for st in range(n_streams):
            w_st = jnp.exp2(m_ref[st] - m_all)
            l_all = l_all + w_st * l_ref[st]
            acc_all = acc_all + _lane_tile(w_st, acc_all.shape[1]) * acc_ref[st]
        ol_ref[0] = acc_all / jnp.sum(l_all, axis=1, keepdims=True)


SAMPLE_SOFTMAX_STREAMS = 2
PAGE_SLOTS = 3


def _mla_attend_sample(qa, qr, ckv_new, kr_new, cache_ckv, cache_krt, page_table, *, tq, pages):
    bsz, n_pages = page_table.shape
    page = cache_ckv.shape[1]
    assert n_pages % pages == 0 and pages % SAMPLE_SOFTMAX_STREAMS == 0
    assert tq & (tq - 1) == 0 and ckv_new.shape[1] == page
    m = MLA_H * tq
    q_spec = pl.BlockSpec((1, m, MLA_KV_RANK), lambda b, j, pt: (b, 0, 0))
    r_spec = pl.BlockSpec((1, m, MLA_ROPE), lambda b, j, pt: (b, 0, 0))
    hbm_spec = pl.BlockSpec(memory_space=pl.ANY)
    grid_spec = pltpu.PrefetchScalarGridSpec(
        num_scalar_prefetch=1,
        grid=(bsz, n_pages // pages),
        in_specs=[q_spec, r_spec,
                  pl.BlockSpec((1, page, MLA_KV_RANK), lambda b, j, pt: (b, 0, 0)),
                  pl.BlockSpec((1, page, MLA_ROPE), lambda b, j, pt: (b, 0, 0)),
                  hbm_spec, hbm_spec],
        out_specs=q_spec,
        scratch_shapes=[pltpu.VMEM((PAGE_SLOTS, pages, page, MLA_KV_RANK), cache_ckv.dtype),
                        pltpu.VMEM((PAGE_SLOTS, pages, MLA_ROPE, page), cache_krt.dtype),
                        pltpu.SemaphoreType.DMA((PAGE_SLOTS,)),
                        pltpu.VMEM((SAMPLE_SOFTMAX_STREAMS, m, LANES), F32),
                        pltpu.VMEM((SAMPLE_SOFTMAX_STREAMS, m, LANES), F32),
                        pltpu.VMEM((SAMPLE_SOFTMAX_STREAMS, m, MLA_KV_RANK), F32)],
    )
    return pl.pallas_call(
        functools.partial(_mla_attend_sample_kernel, tq=tq, pages=pages),
        grid_spec=grid_spec,
        out_shape=jax.ShapeDtypeStruct(qa.shape, F32),
        compiler_params=_params(2),
        name="mla_attend_sample",
    )(page_table, qa, qr, ckv_new, kr_new, cache_ckv, cache_krt)


def _mla_out_kernel(x_ref, ol_ref, wuv_ref, wo_ref, gpost_ref, y_ref, oc_ref):
    y_ref[...] = _mla_out_heads(x_ref[...], lambda hd: ol_ref[hd], oc_ref, wuv_ref, wo_ref, gpost_ref)


def _mla_out(x, ol, w, *, tm):
    rows, d = x.shape
    assert rows % tm == 0
    x_spec = pl.BlockSpec((tm, d), lambda i: (i, 0))
    return pl.pallas_call(
        _mla_out_kernel,
        grid=(rows // tm,),
        in_specs=[x_spec, pl.BlockSpec((MLA_H, tm, MLA_KV_RANK), lambda i: (0, i, 0))]
        + _mla_weight_specs(w, MLA_O_KEYS),
        out_specs=x_spec,
        out_shape=jax.ShapeDtypeStruct(x.shape, F32),
        scratch_shapes=[pltpu.VMEM((tm, MLA_H * MLA_V), F32)],
        compiler_params=_params(1),
        name="mla_out",
    )(x, ol, *[w[k] for k in MLA_O_KEYS])


def _row(v):
    return v.reshape(1, -1).astype(F32)


def _rot_cols(w):
    half = w.shape[-1] // 2
    return jnp.concatenate([-w[..., half:], w[..., :half]], axis=-1)


def _prep_weights(p):
    gla, mla = [], []
    for l in range(N_GLA):
        rank = p["gla_w_gk1"].shape[-1]
        gla.append(dict(
            gpre=_row(p["norm_mix_pre"][l]), gpost=_row(p["norm_mix_post"][l]),
            wqkvr=jnp.concatenate([p["gla_w_q"][l], p["gla_w_k"][l], p["gla_w_v"][l], p["gla_w_r"][l]],
                                  axis=1).astype(BF16),
            wg1=jnp.pad(p["gla_w_gk1"][l], ((0, 0), (0, LANES - rank))).astype(BF16),
            wg2=jnp.pad(p["gla_w_gk2"][l], ((0, LANES - rank), (0, 0))).astype(BF16),
            bgk=_row(p["gla_b_gk"][l]), gnorm=_row(p["gla_norm"][l]), wo=p["gla_w_o"][l].astype(BF16)))
    ffn = dict(
        gpre=p["norm_ffn_pre"][:, None, :].astype(F32), gpost=p["norm_ffn_post"][:, None, :].astype(F32),
        wg=p["ffn_w_gate"].astype(BF16), wu=p["ffn_w_up"].astype(BF16),
        cw=p["ffn_conv_w"].astype(F32), cb=p["ffn_conv_b"][:, None, :].astype(F32),
        wd=p["ffn_w_down"].astype(BF16))
    wuk = jnp.transpose(p["mla_w_uk"], (1, 0, 2)).astype(BF16)
    wuv = jnp.transpose(p["mla_w_uv"], (1, 0, 2)).astype(BF16)
    for j in range(DEPTH - N_GLA):
        l = N_GLA + j
        wuq = p["mla_w_uq"][j].reshape(MLA_Q_RANK, MLA_H, MLA_NOPE + MLA_ROPE)
        nope = wuq[:, :, :MLA_NOPE].reshape(MLA_Q_RANK, MLA_H * MLA_NOPE)
        rope = wuq[:, :, MLA_NOPE:]
        lane_pad = ((0, 0), (0, 0), (0, LANES - MLA_ROPE))
        rope_p = jnp.pad(rope, lane_pad).reshape(MLA_Q_RANK, MLA_H * LANES)
        rot_p = jnp.pad(_rot_cols(rope), lane_pad).reshape(MLA_Q_RANK, MLA_H * LANES)
        mla.append(dict(
            gpre=_row(p["norm_mix_pre"][l]), gpost=_row(p["norm_mix_post"][l]),
            wdq=p["mla_w_dq"][j].astype(BF16), qnorm=_row(p["mla_q_norm"][j]),
            wuq=jnp.concatenate([nope, rope_p, rot_p], axis=1).astype(BF16),
            wuk=wuk, wuv=wuv, wo=p["mla_w_o"][j].astype(BF16)))
    kv = dict(gsrc=_row(p["mla_kv_src_norm"]), wdkv=p["mla_w_dkv"].astype(BF16), gkv=_row(p["mla_kv_norm"]),
              wkr=p["mla_w_kr"].astype(BF16), wkrot=_rot_cols(p["mla_w_kr"]).astype(BF16))
    return gla, ffn, mla, kv


def _rope_tables(pos):
    half = MLA_ROPE // 2
    inv = ROPE_THETA ** (-jnp.arange(half, dtype=F32) / half)
    ang = pos.astype(F32)[:, None] * inv[None, :]
    cos, sin = jnp.cos(ang), jnp.sin(ang)
    return jnp.concatenate([cos, cos], axis=-1), jnp.concatenate([sin, sin], axis=-1)


def _trunk_prompt(x, weights):
    gla_w, ffn_w, mla_w, kv_w = weights
    bsz, t, d = x.shape
    x = x.reshape(bsz * t, d)
    cos, sin = _rope_tables(jnp.arange(t, dtype=jnp.int32))
    cos2, sin2 = jnp.concatenate([cos, cos], -1), jnp.concatenate([sin, sin], -1)
    s0 = jnp.zeros((1, bsz, GLA_H, GLA_DK, GLA_DV), F32)
    buf0 = jnp.zeros((bsz, CONV_W - 1, FFN_DIM), F32)
    tm_gla, tm_ffn, tm_kv, tq = 512, 512, 1024, 256
    states, bufs = [], []
    ckv = kr = ckv_b = kr_b = None
    for l in range(DEPTH):
        if l < N_GLA:
            x, s = _gla_layer(x, s0, 0, gla_w[l], tm=tm_gla, cp=GLA_CHUNK, n_valid=GLA_CHUNK,
                              tiles_per_state=t // tm_gla)
            states.append(s)
        else:
            if l == N_GLA:
                ckv, kr, ckv_b, kr_b = _kv_proj(x, cos, sin, kv_w, tm=tm_kv)
            x = _mla_prompt(x, ckv_b, kr_b, cos2, sin2, mla_w[l - N_GLA], tq=tq, seq=t)
        x, b = _ffn_seq(x, buf0, ffn_w, l, tm=tm_ffn, tiles_per_seq=t // tm_ffn)
        bufs.append(b)
    return (x.reshape(bsz, t, d), jnp.stack(states), jnp.stack(bufs),
            ckv.reshape(bsz, t, -1), kr.reshape(bsz, t, -1))


def _heads_to_seq(q, bsz, t):
    h, _, n = q.shape
    return q.reshape(h, bsz, -1, n)[:, :, :t].transpose(1, 0, 2, 3).reshape(bsz, h * t, n)


def _seq_to_heads(o, tp):
    bsz, m, n = o.shape
    t = m // MLA_H
    o = o.reshape(bsz, MLA_H, t, n).transpose(1, 0, 2, 3)
    return jnp.pad(o, ((0, 0), (0, 0), (0, tp - t), (0, 0))).reshape(MLA_H, bsz * tp, n)


def _trunk_sample(x, state_gla, state_conv, cache_ckv, cache_kr, page_table, weights):
    gla_w, ffn_w, mla_w, kv_w = weights
    bsz, t, d = x.shape
    tp = -(-t // SUBLANES) * SUBLANES
    assert tp & (tp - 1) == 0 and t >= CONV_W - 1
    rows = bsz * tp
    page = cache_ckv.shape[1]
    past = page_table.shape[1] * page
    x = jnp.pad(x, ((0, 0), (0, tp - t), (0, 0))).reshape(rows, d)
    cos, sin = _rope_tables(past + jnp.arange(tp, dtype=jnp.int32))
    cos, sin = jnp.tile(cos, (bsz, 1)), jnp.tile(sin, (bsz, 1))
    cos2, sin2 = jnp.concatenate([cos, cos], -1), jnp.concatenate([sin, sin], -1)
    cache_krt = jnp.swapaxes(cache_kr, 1, 2)
    tm_gla = 8 * tp
    pad_tail = ((0, 0), (0, tp - (CONV_W - 1)), (0, 0))
    states, bufs = [], []
    ckv = kr = ckv_pad = kr_pad = None
    for l in range(DEPTH):
        if l < N_GLA:
            x, s = _gla_layer(x, state_gla, l, gla_w[l], tm=tm_gla, cp=tp, n_valid=t, tiles_per_state=0)
            states.append(s)
        else:
            w = mla_w[l - N_GLA]
            if l == N_GLA:
                ckv, kr, _, _ = _kv_proj(x, cos, sin, kv_w, tm=rows)
                ckv = ckv.reshape(bsz, tp, -1)
                kr = kr.reshape(bsz, tp, -1)
                ckv_pad = jnp.pad(ckv, ((0, 0), (0, page - tp), (0, 0)))
                kr_pad = jnp.pad(kr, ((0, 0), (0, page - tp), (0, 0)))
            qa, qr = _mla_q(x, cos2, sin2, w, tm=rows)
            ol = _mla_attend_sample(_heads_to_seq(qa, bsz, t), _heads_to_seq(qr, bsz, t), ckv_pad, kr_pad,
                                    cache_ckv, cache_krt, page_table, tq=t, pages=32)
            x = _mla_out(x, _seq_to_heads(ol, tp), w, tm=rows)
        buf = state_conv[l]
        prev1 = jnp.pad(buf[:, CONV_W - 2:], ((0, 0), (0, tp - 1), (0, 0))).reshape(rows, FFN_DIM)
        prev2 = jnp.pad(buf, pad_tail).reshape(rows, FFN_DIM)
        x, u = _ffn_rows(x, prev1, prev2, ffn_w, l, tm=rows, tp=tp)
        bufs.append(u.reshape(bsz, tp, FFN_DIM)[:, t - (CONV_W - 1):t])
    return (x.reshape(bsz, tp, d)[:, :t], jnp.stack(states), jnp.stack(bufs), ckv[:, :t], kr[:, :t])


def kernel(x_prompt, x_sample, state_gla, state_ffn_conv, cache_ckv, cache_krope, page_table, norm_mix_pre, norm_mix_post, norm_ffn_pre, norm_ffn_post, ffn_w_gate, ffn_w_up, ffn_conv_w, ffn_conv_b, ffn_w_down, gla_w_q, gla_w_k, gla_w_v, gla_w_gk1, gla_w_gk2, gla_b_gk, gla_w_r, gla_norm, gla_w_o, mla_kv_src_norm, mla_w_dkv, mla_kv_norm, mla_w_kr, mla_w_uk, mla_w_uv, mla_w_dq, mla_q_norm, mla_w_uq, mla_w_o):
    p = dict(norm_mix_pre=norm_mix_pre, norm_mix_post=norm_mix_post, norm_ffn_pre=norm_ffn_pre,
             norm_ffn_post=norm_ffn_post, ffn_w_gate=ffn_w_gate, ffn_w_up=ffn_w_up, ffn_conv_w=ffn_conv_w,
             ffn_conv_b=ffn_conv_b, ffn_w_down=ffn_w_down, gla_w_q=gla_w_q, gla_w_k=gla_w_k, gla_w_v=gla_w_v,
             gla_w_gk1=gla_w_gk1, gla_w_gk2=gla_w_gk2, gla_b_gk=gla_b_gk, gla_w_r=gla_w_r, gla_norm=gla_norm,
             gla_w_o=gla_w_o, mla_kv_src_norm=mla_kv_src_norm, mla_w_dkv=mla_w_dkv, mla_kv_norm=mla_kv_norm,
             mla_w_kr=mla_w_kr, mla_w_uk=mla_w_uk, mla_w_uv=mla_w_uv, mla_w_dq=mla_w_dq, mla_q_norm=mla_q_norm,
             mla_w_uq=mla_w_uq, mla_w_o=mla_w_o)
    weights = _prep_weights(p)
    y_p, gla_p, conv_p, ckv_p, kr_p = _trunk_prompt(x_prompt, weights)
    y_s, gla_s, conv_s, ckv_s, kr_s = _trunk_sample(x_sample, state_gla, state_ffn_conv, cache_ckv,
                                                    cache_krope, page_table, weights)
    return (y_p, y_s, gla_p, gla_s, conv_p, conv_s, ckv_p, ckv_s, kr_p, kr_s)
```

```python
import functools

import jax
import jax.numpy as jnp
from jax import lax
from jax.experimental import pallas as pl
from jax.experimental.pallas import tpu as pltpu

D_MODEL = 1024
DEPTH = 4
N_GLA = DEPTH // 2
GLA_H = 4
GLA_DK = 128
GLA_DV = 256
GLA_GATE_TEMP = 16.0
GLA_CHUNK = 64
GLA_SUBTILE = 256
MLA_H = 8
MLA_NOPE = 128
MLA_ROPE = 64
MLA_V = 128
MLA_Q_RANK = 384
MLA_KV_RANK = 256
ROPE_THETA = 10000.0
FFN_DIM = 2816
CONV_W = 3
EPS = 1e-6
LANES = 128
SUBLANES = 8
MXU_DIM = 256
VMEM_LIMIT = 56 * 1024 * 1024

F32 = jnp.float32
BF16 = jnp.bfloat16
NT_DIMS = (((1,), (1,)), ((), ()))
TN_DIMS = (((0,), (0,)), ((), ()))
LOG2_E = 1.4426950408889634
SM_SCALE_LOG2 = (MLA_NOPE + MLA_ROPE) ** -0.5 * LOG2_E


def _rms(x, g):
    return x * lax.rsqrt(jnp.mean(x * x, axis=-1, keepdims=True) + EPS) * g


def _dot(a, b):
    return jnp.dot(a, b, preferred_element_type=F32)


def _dot_nt(a, b):
    return lax.dot_general(a, b, NT_DIMS, preferred_element_type=F32)


def _log_sigmoid(z):
    return jnp.minimum(z, 0.0) - jnp.log1p(jnp.exp(-jnp.abs(z)))


def _gelu_tanh(x):
    cdf = 0.5 * (1.0 + jnp.tanh(0.7978845608028654 * (x + 0.044715 * (x * x * x))))
    return x * cdf


def _const_spec(shape):
    nd = len(shape)
    return pl.BlockSpec(shape, lambda *_: (0,) * nd, pipeline_mode=pl.Buffered(1))


def _params(n_axes):
    return pltpu.CompilerParams(dimension_semantics=("arbitrary",) * n_axes,
                                vmem_limit_bytes=VMEM_LIMIT)


def _gla_kernel(x_ref, s0_ref, gpre_ref, wqkvr_ref, wg1_ref, wg2_ref, bgk_ref, gn_ref, wo_ref,
                gpost_ref, y_ref, s_ref, proj_ref, lg_ref, o_ref, *, tm, cp, n_valid, tiles_per_state):
    if tiles_per_state:
        @pl.when(pl.program_id(0) % tiles_per_state == 0)
        def _():
            s_ref[...] = s0_ref[...]
    else:
        s_ref[...] = s0_ref[...]

    x = x_ref[...]
    h = _rms(x, gpre_ref[...]).astype(BF16)
    g1 = _dot(h, wg1_ref[...])
    z = _dot(g1.astype(BF16), wg2_ref[...]) + bgk_ref[...]
    lg = _log_sigmoid(z) / GLA_GATE_TEMP
    if n_valid < cp:
        t_in_chunk = lax.broadcasted_iota(jnp.int32, (tm, 1), 0) & (cp - 1)
        lg = jnp.where(t_in_chunk < n_valid, lg, 0.0)
    lg_ref[...] = lg
    proj_ref[...] = _dot(h, wqkvr_ref[...])

    ts = min(tm, GLA_SUBTILE)
    nc = ts // cp
    r_i = lax.broadcasted_iota(jnp.int32, (ts, ts), 0)
    c_i = lax.broadcasted_iota(jnp.int32, (ts, ts), 1)
    intra = (r_i >= c_i) & ((r_i & -cp) == (c_i & -cp))
    t_in = lax.broadcasted_iota(jnp.int32, (ts, GLA_DK), 0) & (cp - 1)
    eye =(lax.broadcasted_iota(jnp.int32, (GLA_DK, GLA_DK), 0)
           == lax.broadcasted_iota(jnp.int32, (GLA_DK, GLA_DK), 1))
    k_off = GLA_H * GLA_DK
    v_off = 2 * GLA_H * GLA_DK
    r_off = v_off + GLA_H * GLA_DV

    for hd in range(GLA_H):
        kcol = slice(hd * GLA_DK, (hd + 1) * GLA_DK)
        s = s_ref[0, hd]
        for sub in range(tm // ts):
            r0 = sub * ts
            b = lg_ref[r0:r0 + ts, kcol]
            shift = 1
            while shift < cp:
                b = b + jnp.where(t_in >= shift, pltpu.roll(b, shift, 0), 0.0)
                shift *= 2
            bl = jnp.concatenate(
                [jnp.broadcast_to(b[(c + 1) * cp - 1:(c + 1) * cp, :], (cp, GLA_DK)) for c in range(nc)], axis=0)
            q = proj_ref[r0:r0 + ts, hd * GLA_DK:(hd + 1) * GLA_DK] * (GLA_DK ** -0.5)
            k = proj_ref[r0:r0 + ts, k_off + hd * GLA_DK:k_off + (hd + 1) * GLA_DK]
            v = proj_ref[r0:r0 + ts, v_off + hd * GLA_DV:v_off + (hd + 1) * GLA_DV]
            qe = q * jnp.exp(b)
            ke = k * jnp.exp(-b)
            kd = k * jnp.exp(bl - b)
            a = jnp.where(intra, _dot_nt(qe.astype(BF16), ke.astype(BF16)), 0.0).astype(BF16)
            o_intra = _dot(a, v.astype(BF16))
            for c in range(nc):
                rows = slice(c * cp, (c + 1) * cp)
                si = sub * nc + c
                if not tiles_per_state:
                    s = s_ref[si, hd]
                o_ref[r0 + c * cp:r0 + (c + 1) * cp, hd * GLA_DV:(hd + 1) * GLA_DV] = (
                    o_intra[rows, :] + _dot(qe[rows, :].astype(BF16), s.astype(BF16)))
                dcol = jnp.sum(jnp.where(eye, jnp.exp(bl[c * cp:c * cp + 1, :]), 0.0), axis=1, keepdims=True)
                s = dcol * s + lax.dot_general(kd[rows, :].astype(BF16), v[rows, :].astype(BF16), TN_DIMS,
                                               preferred_element_type=F32)
                if not tiles_per_state:
                    s_ref[si, hd] = s
        if tiles_per_state:
            s_ref[0, hd] = s

    gn = gn_ref[...]
    for hd in range(GLA_H):
        vcol = slice(hd * GLA_DV, (hd + 1) * GLA_DV)
        r = proj_ref[:, r_off + hd * GLA_DV:r_off + (hd + 1) * GLA_DV]
        o_ref[:, vcol] = _rms(o_ref[:, vcol], gn) * (r * jax.nn.sigmoid(r))
    mix = _dot(o_ref[...].astype(BF16), wo_ref[...])
    y_ref[...] = x + _rms(mix, gpost_ref[...])


def _gla_layer(x, s0_all, layer, w, *, tm, cp, n_valid, tiles_per_state):
    rows, d = x.shape
    assert rows % tm == 0 and tm % cp == 0 and cp & (cp - 1) == 0
    n_proj = w["wqkvr"].shape[1]
    x_spec = pl.BlockSpec((tm, d), lambda i: (i, 0))
    n_blk = 1 if tiles_per_state else tm // cp
    per = tiles_per_state or 1
    s_in_spec = pl.BlockSpec((None, n_blk, GLA_H, GLA_DK, GLA_DV), lambda i: (layer, i // per, 0, 0, 0))
    s_spec = pl.BlockSpec((n_blk, GLA_H, GLA_DK, GLA_DV), lambda i: (i // per, 0, 0, 0))
    return pl.pallas_call(
        functools.partial(_gla_kernel, tm=tm, cp=cp, n_valid=n_valid, tiles_per_state=tiles_per_state),
        grid=(rows // tm,),
        in_specs=[x_spec, s_in_spec, _const_spec((1, d)), _const_spec(w["wqkvr"].shape),
                  _const_spec(w["wg1"].shape), _const_spec(w["wg2"].shape), _const_spec(w["bgk"].shape),
                  _const_spec(w["gnorm"].shape), _const_spec(w["wo"].shape), _const_spec((1, d))],
        out_specs=[x_spec, s_spec],
        out_shape=[jax.ShapeDtypeStruct(x.shape, F32), jax.ShapeDtypeStruct(s0_all.shape[1:], F32)],
        scratch_shapes=[pltpu.VMEM((tm, n_proj), F32), pltpu.VMEM((tm, GLA_H * GLA_DK), F32),
                        pltpu.VMEM((tm, GLA_H * GLA_DV), F32)],
        compiler_params=_params(1),
        name="gla_layer",
    )(x, s0_all, w["gpre"], w["wqkvr"], w["wg1"], w["wg2"], w["bgk"], w["gnorm"], w["wo"], w["gpost"])


def _ffn_core(x, gpre_ref, wg_ref, wu_ref, cw_ref, cb_ref, wd_ref, gpost_ref, shifted, emit_u):
    h = _rms(x, gpre_ref[...]).astype(BF16)
    u = _dot(h, wg_ref[...])
    up = _dot(h, wu_ref[...])
    u1, u2 = shifted(u)
    c = cb_ref[...] + ((cw_ref[0:1, :] * u2 + cw_ref[1:2, :] * u1) + cw_ref[2:3, :] * u)
    act = (_gelu_tanh(c) * up).astype(BF16)
    emit_u(u)
    return x + _rms(_dot(act, wd_ref[...]), gpost_ref[...])


def _ffn_seq_kernel(x_ref, buf_ref, gpre_ref, wg_ref, wu_ref, cw_ref, cb_ref, wd_ref, gpost_ref,
                    y_ref, bufn_ref, carry_ref, *, tm, tiles_per_seq):
    @pl.when(pl.program_id(0) % tiles_per_seq == 0)
    def _():
        carry_ref[...] = buf_ref[0]

    row = lax.broadcasted_iota(jnp.int32, (tm, 1), 0)

    def shifted(u):
        p0 = carry_ref[0:1, :]
        p1 = carry_ref[1:2, :]
        u1 = jnp.where(row == 0, p1, pltpu.roll(u, 1, 0))
        u2 = jnp.where(row == 0, p0, jnp.where(row == 1, p1, pltpu.roll(u, 2, 0)))
        return u1, u2

    def emit_u(u):
        tail = u[tm - (CONV_W - 1):tm, :]
        carry_ref[...] = tail
        bufn_ref[0] = tail

    y_ref[...] = _ffn_core(x_ref[...], gpre_ref, wg_ref, wu_ref, cw_ref, cb_ref, wd_ref, gpost_ref,
                           shifted, emit_u)


def _ffn_rows_kernel(x_ref, prev1_ref, prev2_ref, gpre_ref, wg_ref, wu_ref, cw_ref, cb_ref, wd_ref,
                     gpost_ref, y_ref, u_ref, *, tm, tp):
    t = lax.broadcasted_iota(jnp.int32, (tm, 1), 0) & (tp - 1)

    def shifted(u):
        u1 = jnp.where(t == 0, prev1_ref[...], pltpu.roll(u, 1, 0))
        u2 = jnp.where(t < 2, prev2_ref[...], pltpu.roll(u, 2, 0))
        return u1, u2

    def emit_u(u):
        u_ref[...] = u

    y_ref[...] = _ffn_core(x_ref[...], gpre_ref, wg_ref, wu_ref, cw_ref, cb_ref, wd_ref, gpost_ref,
                           shifted, emit_u)


FFN_KEYS = ("gpre", "wg", "wu", "cw", "cb", "wd", "gpost")


def _ffn_weight_specs(w, layer):
    return [pl.BlockSpec((None,) + w[k].shape[1:], lambda *_: (layer, 0, 0), pipeline_mode=pl.Buffered(1))
            for k in FFN_KEYS]


def _ffn_weights(w):
    return [w[k] for k in FFN_KEYS]


def _ffn_seq(x, buf, w, layer, *, tm, tiles_per_seq):
    rows, d = x.shape
    assert rows % (tm * tiles_per_seq) == 0 and tm >= CONV_W - 1
    x_spec = pl.BlockSpec((tm, d), lambda i: (i, 0))
    b_spec = pl.BlockSpec((1, CONV_W - 1, FFN_DIM), lambda i: (i // tiles_per_seq, 0, 0))
    return pl.pallas_call(
        functools.partial(_ffn_seq_kernel, tm=tm, tiles_per_seq=tiles_per_seq),
        grid=(rows // tm,),
        in_specs=[x_spec, b_spec] + _ffn_weight_specs(w, layer),
        out_specs=[x_spec, b_spec],
        out_shape=[jax.ShapeDtypeStruct(x.shape, F32), jax.ShapeDtypeStruct(buf.shape, F32)],
        scratch_shapes=[pltpu.VMEM((CONV_W - 1, FFN_DIM), F32)],
        compiler_params=_params(1),
        name="ffn_seq",
    )(x, buf, *_ffn_weights(w))


def _ffn_rows(x, prev1, prev2, w, layer, *, tm, tp):
    rows, d = x.shape
    assert rows % tm == 0 and tm % tp == 0 and tp & (tp - 1) == 0
    x_spec = pl.BlockSpec((tm, d), lambda i: (i, 0))
    u_spec = pl.BlockSpec((tm, FFN_DIM), lambda i: (i, 0))
    return pl.pallas_call(
        functools.partial(_ffn_rows_kernel, tm=tm, tp=tp),
        grid=(rows // tm,),
        in_specs=[x_spec, u_spec, u_spec] + _ffn_weight_specs(w, layer),
        out_specs=[x_spec, u_spec],
        out_shape=[jax.ShapeDtypeStruct(x.shape, F32), jax.ShapeDtypeStruct((rows, FFN_DIM), F32)],
        compiler_params=_params(1),
        name="ffn_rows",
    )(x, prev1, prev2, *_ffn_weights(w))


def _kv_math(x, cos_ref, sin_ref, gsrc_ref, wdkv_ref, gkv_ref, wkr_ref, wkrot_ref,
             ckv_ref, kr_ref, ckvb_ref, krb_ref):
    hk = _rms(x, gsrc_ref[...]).astype(BF16)
    ckv = _rms(_dot(hk, wdkv_ref[...]), gkv_ref[...])
    kr = _dot(hk, wkr_ref[...]) * cos_ref[...] + _dot(hk, wkrot_ref[...]) * sin_ref[...]
    ckv_ref[...] = ckv
    kr_ref[...] = kr
    ckvb_ref[...] = ckv.astype(BF16)
    krb_ref[...] = kr.astype(BF16)


def _kv_kernel(x_ref, *refs):
    _kv_math(x_ref[...], *refs)


KV_KEYS = ("gsrc", "wdkv", "gkv", "wkr", "wkrot")


def _kv_specs(rows, d, period, tm):
    t_spec = pl.BlockSpec((tm, MLA_ROPE), lambda i: (i % (period // tm), 0))
    c_spec = pl.BlockSpec((tm, MLA_KV_RANK), lambda i: (i, 0))
    r_spec = pl.BlockSpec((tm, MLA_ROPE), lambda i: (i, 0))
    in_specs = [t_spec, t_spec, _const_spec((1, d)), _const_spec((d, MLA_KV_RANK)),
                _const_spec((1, MLA_KV_RANK)), _const_spec((d, MLA_ROPE)), _const_spec((d, MLA_ROPE))]
    out_shape = [jax.ShapeDtypeStruct((rows, MLA_KV_RANK), F32), jax.ShapeDtypeStruct((rows, MLA_ROPE), F32),
                 jax.ShapeDtypeStruct((rows, MLA_KV_RANK), BF16), jax.ShapeDtypeStruct((rows, MLA_ROPE), BF16)]
    return in_specs, [c_spec, r_spec, c_spec, r_spec], out_shape


def _kv_proj(x, cos, sin, w, *, tm):
    rows, d = x.shape
    period = cos.shape[0]
    assert rows % tm == 0 and period % tm == 0
    in_specs, out_specs, out_shape = _kv_specs(rows, d, period, tm)
    return pl.pallas_call(
        _kv_kernel,
        grid=(rows // tm,),
        in_specs=[pl.BlockSpec((tm, d), lambda i: (i, 0))] + in_specs,
        out_specs=out_specs,
        out_shape=out_shape,
        compiler_params=_params(1),
        name="kv_proj",
    )(x, cos, sin, *[w[k] for k in KV_KEYS])


def _mla_q_heads(x, cos, sin, gpre_ref, wdq_ref, qn_ref, wuq_ref, wuk_ref):
    h = _rms(x, gpre_ref[...]).astype(BF16)
    cq = _rms(_dot(h, wdq_ref[...]), qn_ref[...]).astype(BF16)
    q = _dot(cq, wuq_ref[...])
    hw = MLA_H * LANES
    for hd in range(MLA_H):
        qn = q[:, hd * MLA_NOPE:(hd + 1) * MLA_NOPE].astype(BF16)
        qa = _dot_nt(qn, wuk_ref[hd])
        roped = (q[:, hw + hd * LANES:hw + (hd + 1) * LANES] * cos
                 + q[:, 2 * hw + hd * LANES:2 * hw + (hd + 1) * LANES] * sin)
        yield hd, qa, roped[:, :MLA_ROPE]


def _lane_fold(x, op):
    parts = [x[:, c:c + LANES] for c in range(0, x.shape[1], LANES)]
    while len(parts) > 1:
        parts = [op(parts[i], parts[i + 1]) for i in range(0, len(parts) - 1, 2)] + parts[len(parts) & ~1:]
    return parts[0]


def _lane_tile(x, width):
    return x if width == LANES else jnp.concatenate([x] * (width // LANES), axis=1)


def _softmax_step(s, pv_fn, m_ref, l_ref, acc_ref, rows):
    m_prev = m_ref[rows, :]
    m_new = jnp.maximum(m_prev, jnp.max(_lane_fold(s, jnp.maximum), axis=1, keepdims=True))
    alpha = jnp.exp2(m_prev - m_new)
    p = jnp.exp2(s - _lane_tile(m_new, s.shape[1]))
    l_ref[rows, :] = alpha * l_ref[rows, :] + _lane_fold(p, jnp.add)
    acc_ref[rows, :] = _lane_tile(alpha, acc_ref.shape[1]) * acc_ref[rows, :] + pv_fn(p.astype(BF16))
    m_ref[rows, :] = m_new


def _softmax_result(l_ref, acc_ref, rows):
    return acc_ref[rows, :] / jnp.sum(l_ref[rows, :], axis=1, keepdims=True)


def _mla_out_heads(x, ol_fn, oc_ref, wuv_ref, wo_ref, gpost_ref):
    for hd in range(MLA_H):
        oc_ref[:, hd * MLA_V:(hd + 1) * MLA_V] = _dot(ol_fn(hd).astype(BF16), wuv_ref[hd])
    mix = _dot(oc_ref[...].astype(BF16), wo_ref[...])
    return x + _rms(mix, gpost_ref[...])


def _mla_prompt_kernel(x_ref, ckv_ref, kr_ref, cos_ref, sin_ref, gpre_ref, wdq_ref, qn_ref, wuq_ref,
                       wuk_ref, wuv_ref, wo_ref, gpost_ref, y_ref,
                       qa_ref, qr_ref, m_ref, l_ref, acc_ref, oc_ref, *, tq):
    i = pl.program_id(1)
    x = x_ref[...]
    for hd, qa, qr in _mla_q_heads(x, cos_ref[...], sin_ref[...], gpre_ref, wdq_ref, qn_ref, wuq_ref, wuk_ref):
        qa_ref[hd * tq:(hd + 1) * tq, :] = (qa * SM_SCALE_LOG2).astype(BF16)
        qr_ref[hd * tq:(hd + 1) * tq, :] = (qr * SM_SCALE_LOG2).astype(BF16)
    m_ref[...] = jnp.full(m_ref.shape, -jnp.inf, F32)
    l_ref[...] = jnp.zeros(l_ref.shape, F32)
    acc_ref[...] = jnp.zeros(acc_ref.shape, F32)

    every = slice(None)

    def key_block(j, causal):
        k0 = pl.multiple_of(j * tq, tq)
        kc = ckv_ref[pl.ds(k0, tq), :]
        kk = kr_ref[pl.ds(k0, tq), :]
        s = _dot_nt(qa_ref[...], kc) + _dot_nt(qr_ref[...], kk)
        if causal:
            visible = (lax.broadcasted_iota(jnp.int32, s.shape, 1)
                       <= (lax.broadcasted_iota(jnp.int32, s.shape, 0) & (tq - 1)))
            s = jnp.where(visible, s, -jnp.inf)
        _softmax_step(s, lambda p: _dot(p, kc), m_ref, l_ref, acc_ref, every)

    def body(j, carry):
        key_block(j, False)
        return carry

    lax.fori_loop(0, i, body, 0)
    key_block(i, True)

    def ol_fn(hd):
        return _softmax_result(l_ref, acc_ref, slice(hd * tq, (hd + 1) * tq))

    y_ref[...] = _mla_out_heads(x, ol_fn, oc_ref, wuv_ref, wo_ref, gpost_ref)


def _mla_weight_specs(w, keys):
    return [_const_spec(w[k].shape) for k in keys]


MLA_Q_KEYS = ("gpre", "wdq", "qnorm", "wuq", "wuk")
MLA_O_KEYS = ("wuv", "wo", "gpost")


def _mla_prompt(x, ckv_b, kr_b, cos, sin, w, *, tq, seq):
    rows, d = x.shape
    nt = seq // tq
    assert rows % seq == 0 and seq % tq == 0
    m = MLA_H * tq
    x_spec = pl.BlockSpec((tq, d), lambda b, i: (b * nt + i, 0))
    t_spec = pl.BlockSpec((tq, LANES), lambda b, i: (i, 0))
    keys = MLA_Q_KEYS + MLA_O_KEYS
    return pl.pallas_call(
        functools.partial(_mla_prompt_kernel, tq=tq),
        grid=(rows // seq, nt),
        in_specs=[x_spec,
                  pl.BlockSpec((seq, MLA_KV_RANK), lambda b, i: (b, 0)),
                  pl.BlockSpec((seq, MLA_ROPE), lambda b, i: (b, 0)),
                  t_spec, t_spec] + _mla_weight_specs(w, keys),
        out_specs=x_spec,
        out_shape=jax.ShapeDtypeStruct(x.shape, F32),
        scratch_shapes=[pltpu.VMEM((m, MLA_KV_RANK), BF16), pltpu.VMEM((m, MLA_ROPE), BF16),
                        pltpu.VMEM((m, LANES), F32), pltpu.VMEM((m, LANES), F32), pltpu.VMEM((m, MLA_KV_RANK), F32),
                        pltpu.VMEM((tq, MLA_H * MLA_V), F32)],
        compiler_params=_params(2),
        name="mla_prompt",
    )(x, ckv_b, kr_b, cos, sin, *[w[k] for k in keys])


def _mla_q_kernel(x_ref, cos_ref, sin_ref, gpre_ref, wdq_ref, qn_ref, wuq_ref, wuk_ref, qa_ref, qr_ref):
    for hd, qa, qr in _mla_q_heads(x_ref[...], cos_ref[...], sin_ref[...], gpre_ref, wdq_ref, qn_ref,
                                   wuq_ref, wuk_ref):
        qa_ref[hd] = qa
        qr_ref[hd] = qr


def _mla_q(x, cos, sin, w, *, tm):
    rows, d = x.shape
    assert rows % tm == 0
    return pl.pallas_call(
        _mla_q_kernel,
        grid=(rows // tm,),
        in_specs=[pl.BlockSpec((tm, d), lambda i: (i, 0)), pl.BlockSpec((tm, LANES), lambda i: (i, 0)),
                  pl.BlockSpec((tm, LANES), lambda i: (i, 0))] + _mla_weight_specs(w, MLA_Q_KEYS),
        out_specs=[pl.BlockSpec((MLA_H, tm, MLA_KV_RANK), lambda i: (0, i, 0)),
                   pl.BlockSpec((MLA_H, tm, MLA_ROPE), lambda i: (0, i, 0))],
        out_shape=[jax.ShapeDtypeStruct((MLA_H, rows, MLA_KV_RANK), F32),
                   jax.ShapeDtypeStruct((MLA_H, rows, MLA_ROPE), F32)],
        compiler_params=_params(1),
        name="mla_q",
    )(x, cos, sin, *[w[k] for k in MLA_Q_KEYS])


def _mla_attend_sample_kernel(pt_ref, qa_ref, qr_ref, ckvn_ref, krn_ref, ckv_hbm, krt_hbm, ol_ref,
                              ckv_buf, krt_buf, sem, m_ref, l_ref, acc_ref, *, tq, pages):
    b = pl.program_id(0)
    j = pl.program_id(1)
    nj = pl.num_programs(1)
    last = pl.num_programs(0) * nj - 1
    step = b * nj + j
    ahead = PAGE_SLOTS - 1
    slot = lax.rem(step, PAGE_SLOTS)
    every = slice(None)
    n_streams = m_ref.shape[0]
    per_stream = pages // n_streams

    def page_copies(n):
        sl = lax.rem(n, PAGE_SLOTS)
        src = jnp.minimum(n, last)
        bb = src // nj
        jj = src - bb * nj
        out = []
        for i in range(pages):
            pid = pt_ref[bb, jj * pages + i]
            out.append(pltpu.make_async_copy(ckv_hbm.at[pid], ckv_buf.at[sl, i], sem.at[sl]))
            out.append(pltpu.make_async_copy(krt_hbm.at[pid], krt_buf.at[sl, i], sem.at[sl]))
        return out

    @pl.when(step == 0)
    def _():
        for n in range(ahead):
            for c in page_copies(n):
                c.start()

    @pl.when(j == 0)
    def _():
        m_ref[...] = jnp.full(m_ref.shape, -jnp.inf, F32)
        l_ref[...] = jnp.zeros(l_ref.shape, F32)
        acc_ref[...] = jnp.zeros(acc_ref.shape, F32)

    qa = (qa_ref[0] * SM_SCALE_LOG2).astype(BF16)
    qr = (qr_ref[0] * SM_SCALE_LOG2).astype(BF16)
    for c in page_copies(step):
        c.wait()
    for st in range(n_streams):
        ids = range(st * per_stream, (st + 1) * per_stream)
        kcs = [ckv_buf[slot, i].astype(BF16) for i in ids]
        s = jnp.concatenate([_dot_nt(qa, kc) + _dot(qr, krt_buf[slot, i].astype(BF16)) for i, kc in zip(ids, kcs)],
                            axis=1)
        page = kcs[0].shape[0]

        def pv_pages(p, kcs=kcs):
            out = _dot(p[:, :page], kcs[0])
            for i in range(1, len(kcs)):
                out = out + _dot(p[:, i * page:(i + 1) * page], kcs[i])
            return out

        _softmax_step(s, pv_pages, m_ref.at[st], l_ref.at[st], acc_ref.at[st], every)

    for c in page_copies(step + ahead):
        c.start()

    @pl.when(step == last)
    def _():
        for n in range(1, ahead + 1):
            for c in page_copies(last + n):
                c.wait()

    @pl.when(j == nj - 1)
    def _():
        kc = ckvn_ref[0].astype(BF16)
        kk = krn_ref[0].astype(BF16)
        n_new = kc.shape[0]
        qpos = lax.broadcasted_iota(jnp.int32, (qa.shape[0], 1), 0) & (tq - 1)
        kpos = lax.broadcasted_iota(jnp.int32, (1, n_new), 1)
        sn = jnp.where(kpos <= qpos, _dot_nt(qa, kc) + _dot_nt(qr, kk), -jnp.inf)
        _softmax_step(sn, lambda p: _dot(p, kc), m_ref.at[0], l_ref.at[0], acc_ref.at[0], every)
        m_all = m_ref[0]
        for st in range(1, n_streams):
            m_all = jnp.maximum(m_all, m_ref[st])
        l_all = jnp.zeros(l_ref.shape[1:], F32)
        acc_all = jnp.zeros(acc_ref.shape[1:], F32)
        for st in range(n_streams):
            w_st = jnp.exp2(m_ref[st] - m_all)
            l_all = l_all + w_st * l_ref[st]
            acc_all = acc_all + _lane_tile(w_st, acc_all.shape[1]) * acc_ref[st]
        ol_ref[0] = acc_all / jnp.sum(l_all, axis=1, keepdims=True)


SAMPLE_SOFTMAX_STREAMS = 2
PAGE_SLOTS = 3


def _mla_attend_sample(qa, qr, ckv_new, kr_new, cache_ckv, cache_krt, page_table, *, tq, pages):
    bsz, n_pages = page_table.shape
    page = cache_ckv.shape[1]
    assert n_pages % pages == 0 and pages % SAMPLE_SOFTMAX_STREAMS == 0
    assert tq & (tq - 1) == 0 and ckv_new.shape[1] == page
    m = MLA_H * tq
    q_spec = pl.BlockSpec((1, m, MLA_KV_RANK), lambda b, j, pt: (b, 0, 0))
    r_spec = pl.BlockSpec((1, m, MLA_ROPE), lambda b, j, pt: (b, 0, 0))
    hbm_spec = pl.BlockSpec(memory_space=pl.ANY)
    grid_spec = pltpu.PrefetchScalarGridSpec(
        num_scalar_prefetch=1,
        grid=(bsz, n_pages // pages),
        in_specs=[q_spec, r_spec,
                  pl.BlockSpec((1, page, MLA_KV_RANK), lambda b, j, pt: (b, 0, 0)),
                  pl.BlockSpec((1, page, MLA_ROPE), lambda b, j, pt: (b, 0, 0)),
                  hbm_spec, hbm_spec],
        out_specs=q_spec,
        scratch_shapes=[pltpu.VMEM((PAGE_SLOTS, pages, page, MLA_KV_RANK), cache_ckv.dtype),
                        pltpu.VMEM((PAGE_SLOTS, pages, MLA_ROPE, page), cache_krt.dtype),
                        pltpu.SemaphoreType.DMA((PAGE_SLOTS,)),
                        pltpu.VMEM((SAMPLE_SOFTMAX_STREAMS, m, LANES), F32),
                        pltpu.VMEM((SAMPLE_SOFTMAX_STREAMS, m, LANES), F32),
                        pltpu.VMEM((SAMPLE_SOFTMAX_STREAMS, m, MLA_KV_RANK), F32)],
    )
    return pl.pallas_call(
        functools.partial(_mla_attend_sample_kernel, tq=tq, pages=pages),
        grid_spec=grid_spec,
        out_shape=jax.ShapeDtypeStruct(qa.shape, F32),
        compiler_params=_params(2),
        name="mla_attend_sample",
    )(page_table, qa, qr, ckv_new, kr_new, cache_ckv, cache_krt)


def _mla_out_kernel(x_ref, ol_ref, wuv_ref, wo_ref, gpost_ref, y_ref, oc_ref):
    y_ref[...] = _mla_out_heads(x_ref[...], lambda hd: ol_ref[hd], oc_ref, wuv_ref, wo_ref, gpost_ref)


def _mla_out(x, ol, w, *, tm):
    rows, d = x.shape
    assert rows % tm == 0
    x_spec = pl.BlockSpec((tm, d), lambda i: (i, 0))
    return pl.pallas_call(
        _mla_out_kernel,
        grid=(rows // tm,),
        in_specs=[x_spec, pl.BlockSpec((MLA_H, tm, MLA_KV_RANK), lambda i: (0, i, 0))]
        + _mla_weight_specs(w, MLA_O_KEYS),
        out_specs=x_spec,
        out_shape=jax.ShapeDtypeStruct(x.shape, F32),
        scratch_shapes=[pltpu.VMEM((tm, MLA_H * MLA_V), F32)],
        compiler_params=_params(1),
        name="mla_out",
    )(x, ol, *[w[k] for k in MLA_O_KEYS])


def _row(v):
    return v.reshape(1, -1).astype(F32)


def _rot_cols(w):
    half = w.shape[-1] // 2
    return jnp.concatenate([-w[..., half:], w[..., :half]], axis=-1)


def _prep_weights(p):
    gla, mla = [], []
    for l in range(N_GLA):
        rank = p["gla_w_gk1"].shape[-1]
        gla.append(dict(
            gpre=_row(p["norm_mix_pre"][l]), gpost=_row(p["norm_mix_post"][l]),
            wqkvr=jnp.concatenate([p["gla_w_q"][l], p["gla_w_k"][l], p["gla_w_v"][l], p["gla_w_r"][l]],
                                  axis=1).astype(BF16),
            wg1=jnp.pad(p["gla_w_gk1"][l], ((0, 0), (0, LANES - rank))).astype(BF16),
            wg2=jnp.pad(p["gla_w_gk2"][l], ((0, LANES - rank), (0, 0))).astype(BF16),
            bgk=_row(p["gla_b_gk"][l]), gnorm=_row(p["gla_norm"][l]), wo=p["gla_w_o"][l].astype(BF16)))
    ffn = dict(
        gpre=p["norm_ffn_pre"][:, None, :].astype(F32), gpost=p["norm_ffn_post"][:, None, :].astype(F32),
        wg=p["ffn_w_gate"].astype(BF16), wu=p["ffn_w_up"].astype(BF16),
        cw=p["ffn_conv_w"].astype(F32), cb=p["ffn_conv_b"][:, None, :].astype(F32),
        wd=p["ffn_w_down"].astype(BF16))
    wuk = jnp.transpose(p["mla_w_uk"], (1, 0, 2)).astype(BF16)
    wuv = jnp.transpose(p["mla_w_uv"], (1, 0, 2)).astype(BF16)
    for j in range(DEPTH - N_GLA):
        l = N_GLA + j
        wuq = p["mla_w_uq"][j].reshape(MLA_Q_RANK, MLA_H, MLA_NOPE + MLA_ROPE)
        nope = wuq[:, :, :MLA_NOPE].reshape(MLA_Q_RANK, MLA_H * MLA_NOPE)
        rope = wuq[:, :, MLA_NOPE:]
        lane_pad = ((0, 0), (0, 0), (0, LANES - MLA_ROPE))
        rope_p = jnp.pad(rope, lane_pad).reshape(MLA_Q_RANK, MLA_H * LANES)
        rot_p = jnp.pad(_rot_cols(rope), lane_pad).reshape(MLA_Q_RANK, MLA_H * LANES)
        mla.append(dict(
            gpre=_row(p["norm_mix_pre"][l]), gpost=_row(p["norm_mix_post"][l]),
            wdq=p["mla_w_dq"][j].astype(BF16), qnorm=_row(p["mla_q_norm"][j]),
            wuq=jnp.concatenate([nope, rope_p, rot_p], axis=1).astype(BF16),
            wuk=wuk, wuv=wuv, wo=p["mla_w_o"][j].astype(BF16)))
    kv = dict(gsrc=_row(p["mla_kv_src_norm"]), wdkv=p["mla_w_dkv"].astype(BF16), gkv=_row(p["mla_kv_norm"]),
              wkr=p["mla_w_kr"].astype(BF16), wkrot=_rot_cols(p["mla_w_kr"]).astype(BF16))
    return gla, ffn, mla, kv


def _rope_tables(pos):
    half = MLA_ROPE // 2
    inv = ROPE_THETA ** (-jnp.arange(half, dtype=F32) / half)
    ang = pos.astype(F32)[:, None] * inv[None, :]
    cos, sin = jnp.cos(ang), jnp.sin(ang)
    return jnp.concatenate([cos, cos], axis=-1), jnp.concatenate([sin, sin], axis=-1)


def _trunk_prompt(x, weights):
    gla_w, ffn_w, mla_w, kv_w = weights
    bsz, t, d = x.shape
    x = x.reshape(bsz * t, d)
    cos, sin = _rope_tables(jnp.arange(t, dtype=jnp.int32))
    cos2, sin2 = jnp.concatenate([cos, cos], -1), jnp.concatenate([sin, sin], -1)
    s0 = jnp.zeros((1, bsz, GLA_H, GLA_DK, GLA_DV), F32)
    buf0 = jnp.zeros((bsz, CONV_W - 1, FFN_DIM), F32)
    tm_gla, tm_ffn, tm_kv, tq = 512, 512, 1024, 256
    states, bufs = [], []
    ckv = kr = ckv_b = kr_b = None
    for l in range(DEPTH):
        if l < N_GLA:
            x, s = _gla_layer(x, s0, 0, gla_w[l], tm=tm_gla, cp=GLA_CHUNK, n_valid=GLA_CHUNK,
                              tiles_per_state=t // tm_gla)
            states.append(s)
        else:
            if l == N_GLA:
                ckv, kr, ckv_b, kr_b = _kv_proj(x, cos, sin, kv_w, tm=tm_kv)
            x = _mla_prompt(x, ckv_b, kr_b, cos2, sin2, mla_w[l - N_GLA], tq=tq, seq=t)
        x, b = _ffn_seq(x, buf0, ffn_w, l, tm=tm_ffn, tiles_per_seq=t // tm_ffn)
        bufs.append(b)
    return (x.reshape(bsz, t, d), jnp.stack(states), jnp.stack(bufs),
            ckv.reshape(bsz, t, -1), kr.reshape(bsz, t, -1))


def _heads_to_seq(q, bsz, t):
    h, _, n = q.shape
    return q.reshape(h, bsz, -1, n)[:, :, :t].transpose(1, 0, 2, 3).reshape(bsz, h * t, n)


def _seq_to_heads(o, tp):
    bsz, m, n = o.shape
    t = m // MLA_H
    o = o.reshape(bsz, MLA_H, t, n).transpose(1, 0, 2, 3)
    return jnp.pad(o, ((0, 0), (0, 0), (0, tp - t), (0, 0))).reshape(MLA_H, bsz * tp, n)


def _trunk_sample(x, state_gla, state_conv, cache_ckv, cache_kr, page_table, weights):
    gla_w, ffn_w, mla_w, kv_w = weights
    bsz, t, d = x.shape
    tp = -(-t // SUBLANES) * SUBLANES
    assert tp & (tp - 1) == 0 and t >= CONV_W - 1
    rows = bsz * tp
    page = cache_ckv.shape[1]
    past = page_table.shape[1] * page
    x = jnp.pad(x, ((0, 0), (0, tp - t), (0, 0))).reshape(rows, d)
    cos, sin = _rope_tables(past + jnp.arange(tp, dtype=jnp.int32))
    cos, sin = jnp.tile(cos, (bsz, 1)), jnp.tile(sin, (bsz, 1))
    cos2, sin2 = jnp.concatenate([cos, cos], -1), jnp.concatenate([sin, sin], -1)
    cache_krt = jnp.swapaxes(cache_kr, 1, 2)
    tm_gla = 8 * tp
    pad_tail = ((0, 0), (0, tp - (CONV_W - 1)), (0, 0))
    states, bufs = [], []
    ckv = kr = ckv_pad = kr_pad = None
    for l in range(DEPTH):
        if l < N_GLA:
            x, s = _gla_layer(x, state_gla, l, gla_w[l], tm=tm_gla, cp=tp, n_valid=t, tiles_per_state=0)
            states.append(s)
        else:
            w = mla_w[l - N_GLA]
            if l == N_GLA:
                ckv, kr, _, _ = _kv_proj(x, cos, sin, kv_w, tm=rows)
                ckv = ckv.reshape(bsz, tp, -1)
                kr = kr.reshape(bsz, tp, -1)
                ckv_pad = jnp.pad(ckv, ((0, 0), (0, page - tp), (0, 0)))
                kr_pad = jnp.pad(kr, ((0, 0), (0, page - tp), (0, 0)))
            qa, qr = _mla_q(x, cos2, sin2, w, tm=rows)
            ol = _mla_attend_sample(_heads_to_seq(qa, bsz, t), _heads_to_seq(qr, bsz, t), ckv_pad, kr_pad,
                                    cache_ckv, cache_krt, page_table, tq=t, pages=64)
            x = _mla_out(x, _seq_to_heads(ol, tp), w, tm=rows)
        buf = state_conv[l]
        prev1 = jnp.pad(buf[:, CONV_W - 2:], ((0, 0), (0, tp - 1), (0, 0))).reshape(rows, FFN_DIM)
        prev2 = jnp.pad(buf, pad_tail).reshape(rows, FFN_DIM)
        x, u = _ffn_rows(x, prev1, prev2, ffn_w, l, tm=rows, tp=tp)
        bufs.append(u.reshape(bsz, tp, FFN_DIM)[:, t - (CONV_W - 1):t])
    return (x.reshape(bsz, tp, d)[:, :t], jnp.stack(states), jnp.stack(bufs), ckv[:, :t], kr[:, :t])


def kernel(x_prompt, x_sample, state_gla, state_ffn_conv, cache_ckv, cache_krope, page_table, norm_mix_pre, norm_mix_post, norm_ffn_pre, norm_ffn_post, ffn_w_gate, ffn_w_up, ffn_conv_w, ffn_conv_b, ffn_w_down, gla_w_q, gla_w_k, gla_w_v, gla_w_gk1, gla_w_gk2, gla_b_gk, gla_w_r, gla_norm, gla_w_o, mla_kv_src_norm, mla_w_dkv, mla_kv_norm, mla_w_kr, mla_w_uk, mla_w_uv, mla_w_dq, mla_q_norm, mla_w_uq, mla_w_o):
    p = dict(norm_mix_pre=norm_mix_pre, norm_mix_post=norm_mix_post, norm_ffn_pre=norm_ffn_pre,
             norm_ffn_post=norm_ffn_post, ffn_w_gate=ffn_w_gate, ffn_w_up=ffn_w_up, ffn_conv_w=ffn_conv_w,
             ffn_conv_b=ffn_conv_b, ffn_w_down=ffn_w_down, gla_w_q=gla_w_q, gla_w_k=gla_w_k, gla_w_v=gla_w_v,
             gla_w_gk1=gla_w_gk1, gla_w_gk2=gla_w_gk2, gla_b_gk=gla_b_gk, gla_w_r=gla_w_r, gla_norm=gla_norm,
             gla_w_o=gla_w_o, mla_kv_src_norm=mla_kv_src_norm, mla_w_dkv=mla_w_dkv, mla_kv_norm=mla_kv_norm,
             mla_w_kr=mla_w_kr, mla_w_uk=mla_w_uk, mla_w_uv=mla_w_uv, mla_w_dq=mla_w_dq, mla_q_norm=mla_q_norm,
             mla_w_uq=mla_w_uq, mla_w_o=mla_w_o)
    weights = _prep_weights(p)
    y_p, gla_p, conv_p, ckv_p, kr_p = _trunk_prompt(x_prompt, weights)
    y_s, gla_s, conv_s, ckv_s, kr_s = _trunk_sample(x_sample, state_gla, state_ffn_conv, cache_ckv,
                                                    cache_krope, page_table, weights)
    return (y_p, y_s, gla_p, gla_s, conv_p, conv_s, ckv_p, ckv_s, kr_p, kr_s)
```

```python
import functools

import jax
import jax.numpy as jnp
from jax import lax
from jax.experimental import pallas as pl
from jax.experimental.pallas import tpu as pltpu

D_MODEL = 1024
DEPTH = 4
N_GLA = DEPTH // 2
GLA_H = 4
GLA_DK = 128
GLA_DV = 256
GLA_GATE_TEMP = 16.0
GLA_CHUNK = 64
GLA_SUBTILE = 256
MLA_H = 8
MLA_NOPE = 128
MLA_ROPE = 64
MLA_V = 128
MLA_Q_RANK = 384
MLA_KV_RANK = 256
ROPE_THETA = 10000.0
FFN_DIM = 2816
CONV_W = 3
EPS = 1e-6
LANES = 128
SUBLANES = 8
MXU_DIM = 256
VMEM_LIMIT = 56 * 1024 * 1024

F32 = jnp.float32
BF16 = jnp.bfloat16
NT_DIMS = (((1,), (1,)), ((), ()))
TN_DIMS = (((0,), (0,)), ((), ()))
LOG2_E = 1.4426950408889634
SM_SCALE_LOG2 = (MLA_NOPE + MLA_ROPE) ** -0.5 * LOG2_E


def _rms(x, g):
    return x * lax.rsqrt(jnp.mean(x * x, axis=-1, keepdims=True) + EPS) * g


def _dot(a, b):
    return jnp.dot(a, b, preferred_element_type=F32)


def _dot_nt(a, b):
    return lax.dot_general(a, b, NT_DIMS, preferred_element_type=F32)


def _log_sigmoid(z):
    return jnp.minimum(z, 0.0) - jnp.log1p(jnp.exp(-jnp.abs(z)))


def _gelu_tanh(x):
    cdf = 0.5 * (1.0 + jnp.tanh(0.7978845608028654 * (x + 0.044715 * (x * x * x))))
    return x * cdf


def _const_spec(shape):
    nd = len(shape)
    return pl.BlockSpec(shape, lambda *_: (0,) * nd, pipeline_mode=pl.Buffered(1))


def _params(n_axes):
    return pltpu.CompilerParams(dimension_semantics=("arbitrary",) * n_axes,
                                vmem_limit_bytes=VMEM_LIMIT)


def _gla_kernel(x_ref, s0_ref, gpre_ref, wqkvr_ref, wg1_ref, wg2_ref, bgk_ref, gn_ref, wo_ref,
                gpost_ref, y_ref, s_ref, proj_ref, lg_ref, o_ref, *, tm, cp, n_valid, tiles_per_state):
    if tiles_per_state:
        @pl.when(pl.program_id(0) % tiles_per_state == 0)
        def _():
            s_ref[...] = s0_ref[...]
    else:
        s_ref[...] = s0_ref[...]

    x = x_ref[...]
    h = _rms(x, gpre_ref[...]).astype(BF16)
    g1 = _dot(h, wg1_ref[...])
    z = _dot(g1.astype(BF16), wg2_ref[...]) + bgk_ref[...]
    lg = _log_sigmoid(z) / GLA_GATE_TEMP
    if n_valid < cp:
        t_in_chunk = lax.broadcasted_iota(jnp.int32, (tm, 1), 0) & (cp - 1)
        lg = jnp.where(t_in_chunk < n_valid, lg, 0.0)
    lg_ref[...] = lg
    proj_ref[...] = _dot(h, wqkvr_ref[...])

    ts = min(tm, GLA_SUBTILE)
    nc = ts // cp
    r_i = lax.broadcasted_iota(jnp.int32, (ts, ts), 0)
    c_i = lax.broadcasted_iota(jnp.int32, (ts, ts), 1)
    intra = (r_i >= c_i) & ((r_i & -cp) == (c_i & -cp))
    t_in = lax.broadcasted_iota(jnp.int32, (ts, GLA_DK), 0) & (cp - 1)
    eye =(lax.broadcasted_iota(jnp.int32, (GLA_DK, GLA_DK), 0)
           == lax.broadcasted_iota(jnp.int32, (GLA_DK, GLA_DK), 1))
    k_off = GLA_H * GLA_DK
    v_off = 2 * GLA_H * GLA_DK
    r_off = v_off + GLA_H * GLA_DV

    for hd in range(GLA_H):
        kcol = slice(hd * GLA_DK, (hd + 1) * GLA_DK)
        s = s_ref[0, hd]
        for sub in range(tm // ts):
            r0 = sub * ts
            b = lg_ref[r0:r0 + ts, kcol]
            shift = 1
            while shift < cp:
                b = b + jnp.where(t_in >= shift, pltpu.roll(b, shift, 0), 0.0)
                shift *= 2
            bl = jnp.concatenate(
                [jnp.broadcast_to(b[(c + 1) * cp - 1:(c + 1) * cp, :], (cp, GLA_DK)) for c in range(nc)], axis=0)
            q = proj_ref[r0:r0 + ts, hd * GLA_DK:(hd + 1) * GLA_DK] * (GLA_DK ** -0.5)
            k = proj_ref[r0:r0 + ts, k_off + hd * GLA_DK:k_off + (hd + 1) * GLA_DK]
            v = proj_ref[r0:r0 + ts, v_off + hd * GLA_DV:v_off + (hd + 1) * GLA_DV]
            qe = q * jnp.exp(b)
            ke = k * jnp.exp(-b)
            kd = k * jnp.exp(bl - b)
            a = jnp.where(intra, _dot_nt(qe.astype(BF16), ke.astype(BF16)), 0.0).astype(BF16)
            o_intra = _dot(a, v.astype(BF16))
            for c in range(nc):
                rows = slice(c * cp, (c + 1) * cp)
                si = sub * nc + c
                if not tiles_per_state:
                    s = s_ref[si, hd]
                o_ref[r0 + c * cp:r0 + (c + 1) * cp, hd * GLA_DV:(hd + 1) * GLA_DV] = (
                    o_intra[rows, :] + _dot(qe[rows, :].astype(BF16), s.astype(BF16)))
                dcol = jnp.sum(jnp.where(eye, jnp.exp(bl[c * cp:c * cp + 1, :]), 0.0), axis=1, keepdims=True)
                s = dcol * s + lax.dot_general(kd[rows, :].astype(BF16), v[rows, :].astype(BF16), TN_DIMS,
                                               preferred_element_type=F32)
                if not tiles_per_state:
                    s_ref[si, hd] = s
        if tiles_per_state:
            s_ref[0, hd] = s

    gn = gn_ref[...]
    for hd in range(GLA_H):
        vcol = slice(hd * GLA_DV, (hd + 1) * GLA_DV)
        r = proj_ref[:, r_off + hd * GLA_DV:r_off + (hd + 1) * GLA_DV]
        o_ref[:, vcol] = _rms(o_ref[:, vcol], gn) * (r * jax.nn.sigmoid(r))
    mix = _dot(o_ref[...].astype(BF16), wo_ref[...])
    y_ref[...] = x + _rms(mix, gpost_ref[...])


def _gla_layer(x, s0_all, layer, w, *, tm, cp, n_valid, tiles_per_state):
    rows, d = x.shape
    assert rows % tm == 0 and tm % cp == 0 and cp & (cp - 1) == 0
    n_proj = w["wqkvr"].shape[1]
    x_spec = pl.BlockSpec((tm, d), lambda i: (i, 0))
    n_blk = 1 if tiles_per_state else tm // cp
    per = tiles_per_state or 1
    s_in_spec = pl.BlockSpec((None, n_blk, GLA_H, GLA_DK, GLA_DV), lambda i: (layer, i // per, 0, 0, 0))
    s_spec = pl.BlockSpec((n_blk, GLA_H, GLA_DK, GLA_DV), lambda i: (i // per, 0, 0, 0))
    return pl.pallas_call(
        functools.partial(_gla_kernel, tm=tm, cp=cp, n_valid=n_valid, tiles_per_state=tiles_per_state),
        grid=(rows // tm,),
        in_specs=[x_spec, s_in_spec, _const_spec((1, d)), _const_spec(w["wqkvr"].shape),
                  _const_spec(w["wg1"].shape), _const_spec(w["wg2"].shape), _const_spec(w["bgk"].shape),
                  _const_spec(w["gnorm"].shape), _const_spec(w["wo"].shape), _const_spec((1, d))],
        out_specs=[x_spec, s_spec],
        out_shape=[jax.ShapeDtypeStruct(x.shape, F32), jax.ShapeDtypeStruct(s0_all.shape[1:], F32)],
        scratch_shapes=[pltpu.VMEM((tm, n_proj), F32), pltpu.VMEM((tm, GLA_H * GLA_DK), F32),
                        pltpu.VMEM((tm, GLA_H * GLA_DV), F32)],
        compiler_params=_params(1),
        name="gla_layer",
    )(x, s0_all, w["gpre"], w["wqkvr"], w["wg1"], w["wg2"], w["bgk"], w["gnorm"], w["wo"], w["gpost"])


def _ffn_core(x, gpre_ref, wg_ref, wu_ref, cw_ref, cb_ref, wd_ref, gpost_ref, shifted, emit_u):
    h = _rms(x, gpre_ref[...]).astype(BF16)
    u = _dot(h, wg_ref[...])
    up = _dot(h, wu_ref[...])
    u1, u2 = shifted(u)
    c = cb_ref[...] + ((cw_ref[0:1, :] * u2 + cw_ref[1:2, :] * u1) + cw_ref[2:3, :] * u)
    act = (_gelu_tanh(c) * up).astype(BF16)
    emit_u(u)
    return x + _rms(_dot(act, wd_ref[...]), gpost_ref[...])


def _ffn_seq_kernel(x_ref, buf_ref, gpre_ref, wg_ref, wu_ref, cw_ref, cb_ref, wd_ref, gpost_ref,
                    y_ref, bufn_ref, carry_ref, *, tm, tiles_per_seq):
    @pl.when(pl.program_id(0) % tiles_per_seq == 0)
    def _():
        carry_ref[...] = buf_ref[0]

    row = lax.broadcasted_iota(jnp.int32, (tm, 1), 0)

    def shifted(u):
        p0 = carry_ref[0:1, :]
        p1 = carry_ref[1:2, :]
        u1 = jnp.where(row == 0, p1, pltpu.roll(u, 1, 0))
        u2 = jnp.where(row == 0, p0, jnp.where(row == 1, p1, pltpu.roll(u, 2, 0)))
        return u1, u2

    def emit_u(u):
        tail = u[tm - (CONV_W - 1):tm, :]
        carry_ref[...] = tail
        bufn_ref[0] = tail

    y_ref[...] = _ffn_core(x_ref[...], gpre_ref, wg_ref, wu_ref, cw_ref, cb_ref, wd_ref, gpost_ref,
                           shifted, emit_u)


def _ffn_rows_kernel(x_ref, prev1_ref, prev2_ref, gpre_ref, wg_ref, wu_ref, cw_ref, cb_ref, wd_ref,
                     gpost_ref, y_ref, u_ref, *, tm, tp):
    t = lax.broadcasted_iota(jnp.int32, (tm, 1), 0) & (tp - 1)

    def shifted(u):
        u1 = jnp.where(t == 0, prev1_ref[...], pltpu.roll(u, 1, 0))
        u2 = jnp.where(t < 2, prev2_ref[...], pltpu.roll(u, 2, 0))
        return u1, u2

    def emit_u(u):
        u_ref[...] = u

    y_ref[...] = _ffn_core(x_ref[...], gpre_ref, wg_ref, wu_ref, cw_ref, cb_ref, wd_ref, gpost_ref,
                           shifted, emit_u)


FFN_KEYS = ("gpre", "wg", "wu", "cw", "cb", "wd", "gpost")


def _ffn_weight_specs(w, layer):
    return [pl.BlockSpec((None,) + w[k].shape[1:], lambda *_: (layer, 0, 0), pipeline_mode=pl.Buffered(1))
            for k in FFN_KEYS]


def _ffn_weights(w):
    return [w[k] for k in FFN_KEYS]


def _ffn_seq(x, buf, w, layer, *, tm, tiles_per_seq):
    rows, d = x.shape
    assert rows % (tm * tiles_per_seq) == 0 and tm >= CONV_W - 1
    x_spec = pl.BlockSpec((tm, d), lambda i: (i, 0))
    b_spec = pl.BlockSpec((1, CONV_W - 1, FFN_DIM), lambda i: (i // tiles_per_seq, 0, 0))
    return pl.pallas_call(
        functools.partial(_ffn_seq_kernel, tm=tm, tiles_per_seq=tiles_per_seq),
        grid=(rows // tm,),
        in_specs=[x_spec, b_spec] + _ffn_weight_specs(w, layer),
        out_specs=[x_spec, b_spec],
        out_shape=[jax.ShapeDtypeStruct(x.shape, F32), jax.ShapeDtypeStruct(buf.shape, F32)],
        scratch_shapes=[pltpu.VMEM((CONV_W - 1, FFN_DIM), F32)],
        compiler_params=_params(1),
        name="ffn_seq",
    )(x, buf, *_ffn_weights(w))


def _ffn_rows(x, prev1, prev2, w, layer, *, tm, tp):
    rows, d = x.shape
    assert rows % tm == 0 and tm % tp == 0 and tp & (tp - 1) == 0
    x_spec = pl.BlockSpec((tm, d), lambda i: (i, 0))
    u_spec = pl.BlockSpec((tm, FFN_DIM), lambda i: (i, 0))
    return pl.pallas_call(
        functools.partial(_ffn_rows_kernel, tm=tm, tp=tp),
        grid=(rows // tm,),
        in_specs=[x_spec, u_spec, u_spec] + _ffn_weight_specs(w, layer),
        out_specs=[x_spec, u_spec],
        out_shape=[jax.ShapeDtypeStruct(x.shape, F32), jax.ShapeDtypeStruct((rows, FFN_DIM), F32)],
        compiler_params=_params(1),
        name="ffn_rows",
    )(x, prev1, prev2, *_ffn_weights(w))


def _kv_math(x, cos_ref, sin_ref, gsrc_ref, wdkv_ref, gkv_ref, wkr_ref, wkrot_ref,
             ckv_ref, kr_ref, ckvb_ref, krb_ref):
    hk = _rms(x, gsrc_ref[...]).astype(BF16)
    ckv = _rms(_dot(hk, wdkv_ref[...]), gkv_ref[...])
    kr = _dot(hk, wkr_ref[...]) * cos_ref[...] + _dot(hk, wkrot_ref[...]) * sin_ref[...]
    ckv_ref[...] = ckv
    kr_ref[...] = kr
    ckvb_ref[...] = ckv.astype(BF16)
    krb_ref[...] = kr.astype(BF16)


def _kv_kernel(x_ref, *refs):
    _kv_math(x_ref[...], *refs)


KV_KEYS = ("gsrc", "wdkv", "gkv", "wkr", "wkrot")


def _kv_specs(rows, d, period, tm):
    t_spec = pl.BlockSpec((tm, MLA_ROPE), lambda i: (i % (period // tm), 0))
    c_spec = pl.BlockSpec((tm, MLA_KV_RANK), lambda i: (i, 0))
    r_spec = pl.BlockSpec((tm, MLA_ROPE), lambda i: (i, 0))
    in_specs = [t_spec, t_spec, _const_spec((1, d)), _const_spec((d, MLA_KV_RANK)),
                _const_spec((1, MLA_KV_RANK)), _const_spec((d, MLA_ROPE)), _const_spec((d, MLA_ROPE))]
    out_shape = [jax.ShapeDtypeStruct((rows, MLA_KV_RANK), F32), jax.ShapeDtypeStruct((rows, MLA_ROPE), F32),
                 jax.ShapeDtypeStruct((rows, MLA_KV_RANK), BF16), jax.ShapeDtypeStruct((rows, MLA_ROPE), BF16)]
    return in_specs, [c_spec, r_spec, c_spec, r_spec], out_shape


def _kv_proj(x, cos, sin, w, *, tm):
    rows, d = x.shape
    period = cos.shape[0]
    assert rows % tm == 0 and period % tm == 0
    in_specs, out_specs, out_shape = _kv_specs(rows, d, period, tm)
    return pl.pallas_call(
        _kv_kernel,
        grid=(rows // tm,),
        in_specs=[pl.BlockSpec((tm, d), lambda i: (i, 0))] + in_specs,
        out_specs=out_specs,
        out_shape=out_shape,
        compiler_params=_params(1),
        name="kv_proj",
    )(x, cos, sin, *[w[k] for k in KV_KEYS])


def _mla_q_heads(x, cos, sin, gpre_ref, wdq_ref, qn_ref, wuq_ref, wuk_ref):
    h = _rms(x, gpre_ref[...]).astype(BF16)
    cq = _rms(_dot(h, wdq_ref[...]), qn_ref[...]).astype(BF16)
    q = _dot(cq, wuq_ref[...])
    hw = MLA_H * LANES
    for hd in range(MLA_H):
        qn = q[:, hd * MLA_NOPE:(hd + 1) * MLA_NOPE].astype(BF16)
        qa = _dot_nt(qn, wuk_ref[hd])
        roped = (q[:, hw + hd * LANES:hw + (hd + 1) * LANES] * cos
                 + q[:, 2 * hw + hd * LANES:2 * hw + (hd + 1) * LANES] * sin)
        yield hd, qa, roped[:, :MLA_ROPE]


def _lane_fold(x, op):
    parts = [x[:, c:c + LANES] for c in range(0, x.shape[1], LANES)]
    while len(parts) > 1:
        parts = [op(parts[i], parts[i + 1]) for i in range(0, len(parts) - 1, 2)] + parts[len(parts) & ~1:]
    return parts[0]


def _lane_tile(x, width):
    return x if width == LANES else jnp.concatenate([x] * (width // LANES), axis=1)


def _softmax_step(s, pv_fn, m_ref, l_ref, acc_ref, rows):
    m_prev = m_ref[rows, :]
    m_new = jnp.maximum(m_prev, jnp.max(_lane_fold(s, jnp.maximum), axis=1, keepdims=True))
    alpha = jnp.exp2(m_prev - m_new)
    p = jnp.exp2(s - _lane_tile(m_new, s.shape[1]))
    l_ref[rows, :] = alpha * l_ref[rows, :] + _lane_fold(p, jnp.add)
    acc_ref[rows, :] = _lane_tile(alpha, acc_ref.shape[1]) * acc_ref[rows, :] + pv_fn(p.astype(BF16))
    m_ref[rows, :] = m_new


def _softmax_result(l_ref, acc_ref, rows):
    return acc_ref[rows, :] / jnp.sum(l_ref[rows, :], axis=1, keepdims=True)


def _mla_out_heads(x, ol_fn, oc_ref, wuv_ref, wo_ref, gpost_ref):
    for hd in range(MLA_H):
        oc_ref[:, hd * MLA_V:(hd + 1) * MLA_V] = _dot(ol_fn(hd).astype(BF16), wuv_ref[hd])
    mix = _dot(oc_ref[...].astype(BF16), wo_ref[...])
    return x + _rms(mix, gpost_ref[...])


def _mla_prompt_kernel(x_ref, ckv_ref, kr_ref, cos_ref, sin_ref, gpre_ref, wdq_ref, qn_ref, wuq_ref,
                       wuk_ref, wuv_ref, wo_ref, gpost_ref, y_ref,
                       qa_ref, qr_ref, m_ref, l_ref, acc_ref, oc_ref, *, tq):
    i = pl.program_id(1)
    x = x_ref[...]
    for hd, qa, qr in _mla_q_heads(x, cos_ref[...], sin_ref[...], gpre_ref, wdq_ref, qn_ref, wuq_ref, wuk_ref):
        qa_ref[hd * tq:(hd + 1) * tq, :] = (qa * SM_SCALE_LOG2).astype(BF16)
        qr_ref[hd * tq:(hd + 1) * tq, :] = (qr * SM_SCALE_LOG2).astype(BF16)
    m_ref[...] = jnp.full(m_ref.shape, -jnp.inf, F32)
    l_ref[...] = jnp.zeros(l_ref.shape, F32)
    acc_ref[...] = jnp.zeros(acc_ref.shape, F32)

    every = slice(None)

    def key_block(j, causal):
        k0 = pl.multiple_of(j * tq, tq)
        kc = ckv_ref[pl.ds(k0, tq), :]
        kk = kr_ref[pl.ds(k0, tq), :]
        s = _dot_nt(qa_ref[...], kc) + _dot_nt(qr_ref[...], kk)
        if causal:
            visible = (lax.broadcasted_iota(jnp.int32, s.shape, 1)
                       <= (lax.broadcasted_iota(jnp.int32, s.shape, 0) & (tq - 1)))
            s = jnp.where(visible, s, -jnp.inf)
        _softmax_step(s, lambda p: _dot(p, kc), m_ref, l_ref, acc_ref, every)

    def body(j, carry):
        key_block(j, False)
        return carry

    lax.fori_loop(0, i, body, 0)
    key_block(i, True)

    def ol_fn(hd):
        return _softmax_result(l_ref, acc_ref, slice(hd * tq, (hd + 1) * tq))

    y_ref[...] = _mla_out_heads(x, ol_fn, oc_ref, wuv_ref, wo_ref, gpost_ref)


def _mla_weight_specs(w, keys):
    return [_const_spec(w[k].shape) for k in keys]


MLA_Q_KEYS = ("gpre", "wdq", "qnorm", "wuq", "wuk")
MLA_O_KEYS = ("wuv", "wo", "gpost")


def _mla_prompt(x, ckv_b, kr_b, cos, sin, w, *, tq, seq):
    rows, d = x.shape
    nt = seq // tq
    assert rows % seq == 0 and seq % tq == 0
    m = MLA_H * tq
    x_spec = pl.BlockSpec((tq, d), lambda b, i: (b * nt + i, 0))
    t_spec = pl.BlockSpec((tq, LANES), lambda b, i: (i, 0))
    keys = MLA_Q_KEYS + MLA_O_KEYS
    return pl.pallas_call(
        functools.partial(_mla_prompt_kernel, tq=tq),
        grid=(rows // seq, nt),
        in_specs=[x_spec,
                  pl.BlockSpec((seq, MLA_KV_RANK), lambda b, i: (b, 0)),
                  pl.BlockSpec((seq, MLA_ROPE), lambda b, i: (b, 0)),
                  t_spec, t_spec] + _mla_weight_specs(w, keys),
        out_specs=x_spec,
        out_shape=jax.ShapeDtypeStruct(x.shape, F32),
        scratch_shapes=[pltpu.VMEM((m, MLA_KV_RANK), BF16), pltpu.VMEM((m, MLA_ROPE), BF16),
                        pltpu.VMEM((m, LANES), F32), pltpu.VMEM((m, LANES), F32), pltpu.VMEM((m, MLA_KV_RANK), F32),
                        pltpu.VMEM((tq, MLA_H * MLA_V), F32)],
        compiler_params=_params(2),
        name="mla_prompt",
    )(x, ckv_b, kr_b, cos, sin, *[w[k] for k in keys])


def _mla_q_kernel(x_ref, cos_ref, sin_ref, gpre_ref, wdq_ref, qn_ref, wuq_ref, wuk_ref, qa_ref, qr_ref):
    for hd, qa, qr in _mla_q_heads(x_ref[...], cos_ref[...], sin_ref[...], gpre_ref, wdq_ref, qn_ref,
                                   wuq_ref, wuk_ref):
        qa_ref[hd] = qa
        qr_ref[hd] = qr


def _mla_q(x, cos, sin, w, *, tm):
    rows, d = x.shape
    assert rows % tm == 0
    return pl.pallas_call(
        _mla_q_kernel,
        grid=(rows // tm,),
        in_specs=[pl.BlockSpec((tm, d), lambda i: (i, 0)), pl.BlockSpec((tm, LANES), lambda i: (i, 0)),
                  pl.BlockSpec((tm, LANES), lambda i: (i, 0))] + _mla_weight_specs(w, MLA_Q_KEYS),
        out_specs=[pl.BlockSpec((MLA_H, tm, MLA_KV_RANK), lambda i: (0, i, 0)),
                   pl.BlockSpec((MLA_H, tm, MLA_ROPE), lambda i: (0, i, 0))],
        out_shape=[jax.ShapeDtypeStruct((MLA_H, rows, MLA_KV_RANK), F32),
                   jax.ShapeDtypeStruct((MLA_H, rows, MLA_ROPE), F32)],
        compiler_params=_params(1),
        name="mla_q",
    )(x, cos, sin, *[w[k] for k in MLA_Q_KEYS])


def _mla_attend_sample_kernel(pt_ref, qa_ref, qr_ref, ckvn_ref, krn_ref, ckv_hbm, krt_hbm, ol_ref,
                              ckv_buf, krt_buf, sem, m_ref, l_ref, acc_ref, *, tq, pages):
    b = pl.program_id(0)
    j = pl.program_id(1)
    nj = pl.num_programs(1)
    last = pl.num_programs(0) * nj - 1
    step = b * nj + j
    ahead = PAGE_SLOTS - 1
    slot = lax.rem(step, PAGE_SLOTS)
    every = slice(None)
    n_streams = m_ref.shape[0]
    per_stream = pages // n_streams

    def page_copies(n):
        sl = lax.rem(n, PAGE_SLOTS)
        src = jnp.minimum(n, last)
        bb = src // nj
        jj = src - bb * nj
        out = []
        for i in range(pages):
            pid = pt_ref[bb, jj * pages + i]
            out.append(pltpu.make_async_copy(ckv_hbm.at[pid], ckv_buf.at[sl, i], sem.at[sl]))
            out.append(pltpu.make_async_copy(krt_hbm.at[pid], krt_buf.at[sl, i], sem.at[sl]))
        return out

    @pl.when(step == 0)
    def _():
        for n in range(ahead):
            for c in page_copies(n):
                c.start()

    @pl.when(j == 0)
    def _():
        m_ref[...] = jnp.full(m_ref.shape, -jnp.inf, F32)
        l_ref[...] = jnp.zeros(l_ref.shape, F32)
        acc_ref[...] = jnp.zeros(acc_ref.shape, F32)

    qa = (qa_ref[0] * SM_SCALE_LOG2).astype(BF16)
    qr = (qr_ref[0] * SM_SCALE_LOG2).astype(BF16)
    for c in page_copies(step):
        c.wait()
    for st in range(n_streams):
        ids = range(st * per_stream, (st + 1) * per_stream)
        kcs = [ckv_buf[slot, i].astype(BF16) for i in ids]
        s = jnp.concatenate([_dot_nt(qa, kc) + _dot(qr, krt_buf[slot, i].astype(BF16)) for i, kc in zip(ids, kcs)],
                            axis=1)
        page = kcs[0].shape[0]

        def pv_pages(p, kcs=kcs):
            out = _dot(p[:, :page], kcs[0])
            for i in range(1, len(kcs)):
                out = out + _dot(p[:, i * page:(i + 1) * page], kcs[i])
            return out

        _softmax_step(s, pv_pages, m_ref.at[st], l_ref.at[st], acc_ref.at[st], every)

    for c in page_copies(step + ahead):
        c.start()

    @pl.when(step == last)
    def _():
        for n in range(1, ahead + 1):
            for c in page_copies(last + n):
                c.wait()

    @pl.when(j == nj - 1)
    def _():
        kc = ckvn_ref[0].astype(BF16)
        kk = krn_ref[0].astype(BF16)
        n_new = kc.shape[0]
        qpos = lax.broadcasted_iota(jnp.int32, (qa.shape[0], 1), 0) & (tq - 1)
        kpos = lax.broadcasted_iota(jnp.int32, (1, n_new), 1)
        sn = jnp.where(kpos <= qpos, _dot_nt(qa, kc) + _dot_nt(qr, kk), -jnp.inf)
        _softmax_step(sn, lambda p: _dot(p, kc), m_ref.at[0], l_ref.at[0], acc_ref.at[0], every)
        m_all = m_ref[0]
        for st in range(1, n_streams):
            m_all = jnp.maximum(m_all, m_ref[st])
        l_all = jnp.zeros(l_ref.shape[1:], F32)
        acc_all = jnp.zeros(acc_ref.shape[1:], F32)
        for st in range(n_streams):
            w_st = jnp.exp2(m_ref[st] - m_all)
            l_all = l_all + w_st * l_ref[st]
            acc_all = acc_all + _lane_tile(w_st, acc_all.shape[1]) * acc_ref[st]
        ol_ref[0] = acc_all / jnp.sum(l_all, axis=1, keepdims=True)


SAMPLE_SOFTMAX_STREAMS = 2
PAGE_SLOTS = 3


def _mla_attend_sample(qa, qr, ckv_new, kr_new, cache_ckv, cache_krt, page_table, *, tq, pages):
    bsz, n_pages = page_table.shape
    page = cache_ckv.shape[1]
    assert n_pages % pages == 0 and pages % SAMPLE_SOFTMAX_STREAMS == 0
    assert tq & (tq - 1) == 0 and ckv_new.shape[1] == page
    m = MLA_H * tq
    q_spec = pl.BlockSpec((1, m, MLA_KV_RANK), lambda b, j, pt: (b, 0, 0))
    r_spec = pl.BlockSpec((1, m, MLA_ROPE), lambda b, j, pt: (b, 0, 0))
    hbm_spec = pl.BlockSpec(memory_space=pl.ANY)
    grid_spec = pltpu.PrefetchScalarGridSpec(
        num_scalar_prefetch=1,
        grid=(bsz, n_pages // pages),
        in_specs=[q_spec, r_spec,
                  pl.BlockSpec((1, page, MLA_KV_RANK), lambda b, j, pt: (b, 0, 0)),
                  pl.BlockSpec((1, page, MLA_ROPE), lambda b, j, pt: (b, 0, 0)),
                  hbm_spec, hbm_spec],
        out_specs=q_spec,
        scratch_shapes=[pltpu.VMEM((PAGE_SLOTS, pages, page, MLA_KV_RANK), cache_ckv.dtype),
                        pltpu.VMEM((PAGE_SLOTS, pages, MLA_ROPE, page), cache_krt.dtype),
                        pltpu.SemaphoreType.DMA((PAGE_SLOTS,)),
                        pltpu.VMEM((SAMPLE_SOFTMAX_STREAMS, m, LANES), F32),
                        pltpu.VMEM((SAMPLE_SOFTMAX_STREAMS, m, LANES), F32),
                        pltpu.VMEM((SAMPLE_SOFTMAX_STREAMS, m, MLA_KV_RANK), F32)],
    )
    return pl.pallas_call(
        functools.partial(_mla_attend_sample_kernel, tq=tq, pages=pages),
        grid_spec=grid_spec,
        out_shape=jax.ShapeDtypeStruct(qa.shape, F32),
        compiler_params=_params(2),
        name="mla_attend_sample",
    )(page_table, qa, qr, ckv_new, kr_new, cache_ckv, cache_krt)


def _mla_out_kernel(x_ref, ol_ref, wuv_ref, wo_ref, gpost_ref, y_ref, oc_ref):
    y_ref[...] = _mla_out_heads(x_ref[...], lambda hd: ol_ref[hd], oc_ref, wuv_ref, wo_ref, gpost_ref)


def _mla_out(x, ol, w, *, tm):
    rows, d = x.shape
    assert rows % tm == 0
    x_spec = pl.BlockSpec((tm, d), lambda i: (i, 0))
    return pl.pallas_call(
        _mla_out_kernel,
        grid=(rows // tm,),
        in_specs=[x_spec, pl.BlockSpec((MLA_H, tm, MLA_KV_RANK), lambda i: (0, i, 0))]
        + _mla_weight_specs(w, MLA_O_KEYS),
        out_specs=x_spec,
        out_shape=jax.ShapeDtypeStruct(x.shape, F32),
        scratch_shapes=[pltpu.VMEM((tm, MLA_H * MLA_V), F32)],
        compiler_params=_params(1),
        name="mla_out",
    )(x, ol, *[w[k] for k in MLA_O_KEYS])


def _row(v):
    return v.reshape(1, -1).astype(F32)


def _rot_cols(w):
    half = w.shape[-1] // 2
    return jnp.concatenate([-w[..., half:], w[..., :half]], axis=-1)


def _prep_weights(p):
    gla, mla = [], []
    for l in range(N_GLA):
        rank = p["gla_w_gk1"].shape[-1]
        gla.append(dict(
            gpre=_row(p["norm_mix_pre"][l]), gpost=_row(p["norm_mix_post"][l]),
            wqkvr=jnp.concatenate([p["gla_w_q"][l], p["gla_w_k"][l], p["gla_w_v"][l], p["gla_w_r"][l]],
                                  axis=1).astype(BF16),
            wg1=jnp.pad(p["gla_w_gk1"][l], ((0, 0), (0, LANES - rank))).astype(BF16),
            wg2=jnp.pad(p["gla_w_gk2"][l], ((0, LANES - rank), (0, 0))).astype(BF16),
            bgk=_row(p["gla_b_gk"][l]), gnorm=_row(p["gla_norm"][l]), wo=p["gla_w_o"][l].astype(BF16)))
    ffn = dict(
        gpre=p["norm_ffn_pre"][:, None, :].astype(F32), gpost=p["norm_ffn_post"][:, None, :].astype(F32),
        wg=p["ffn_w_gate"].astype(BF16), wu=p["ffn_w_up"].astype(BF16),
        cw=p["ffn_conv_w"].astype(F32), cb=p["ffn_conv_b"][:, None, :].astype(F32),
        wd=p["ffn_w_down"].astype(BF16))
    wuk = jnp.transpose(p["mla_w_uk"], (1, 0, 2)).astype(BF16)
    wuv = jnp.transpose(p["mla_w_uv"], (1, 0, 2)).astype(BF16)
    for j in range(DEPTH - N_GLA):
        l = N_GLA + j
        wuq = p["mla_w_uq"][j].reshape(MLA_Q_RANK, MLA_H, MLA_NOPE + MLA_ROPE)
        nope = wuq[:, :, :MLA_NOPE].reshape(MLA_Q_RANK, MLA_H * MLA_NOPE)
        rope = wuq[:, :, MLA_NOPE:]
        lane_pad = ((0, 0), (0, 0), (0, LANES - MLA_ROPE))
        rope_p = jnp.pad(rope, lane_pad).reshape(MLA_Q_RANK, MLA_H * LANES)
        rot_p = jnp.pad(_rot_cols(rope), lane_pad).reshape(MLA_Q_RANK, MLA_H * LANES)
        mla.append(dict(
            gpre=_row(p["norm_mix_pre"][l]), gpost=_row(p["norm_mix_post"][l]),
            wdq=p["mla_w_dq"][j].astype(BF16), qnorm=_row(p["mla_q_norm"][j]),
            wuq=jnp.concatenate([nope, rope_p, rot_p], axis=1).astype(BF16),
            wuk=wuk, wuv=wuv, wo=p["mla_w_o"][j].astype(BF16)))
    kv = dict(gsrc=_row(p["mla_kv_src_norm"]), wdkv=p["mla_w_dkv"].astype(BF16), gkv=_row(p["mla_kv_norm"]),
              wkr=p["mla_w_kr"].astype(BF16), wkrot=_rot_cols(p["mla_w_kr"]).astype(BF16))
    return gla, ffn, mla, kv


def _rope_tables(pos):
    half = MLA_ROPE // 2
    inv = ROPE_THETA ** (-jnp.arange(half, dtype=F32) / half)
    ang = pos.astype(F32)[:, None] * inv[None, :]
    cos, sin = jnp.cos(ang), jnp.sin(ang)
    return jnp.concatenate([cos, cos], axis=-1), jnp.concatenate([sin, sin], axis=-1)


def _trunk_prompt(x, weights):
    gla_w, ffn_w, mla_w, kv_w = weights
    bsz, t, d = x.shape
    x = x.reshape(bsz * t, d)
    cos, sin = _rope_tables(jnp.arange(t, dtype=jnp.int32))
    cos2, sin2 = jnp.concatenate([cos, cos], -1), jnp.concatenate([sin, sin], -1)
    s0 = jnp.zeros((1, bsz, GLA_H, GLA_DK, GLA_DV), F32)
    buf0 = jnp.zeros((bsz, CONV_W - 1, FFN_DIM), F32)
    tm_gla, tm_ffn, tm_kv, tq = 1024, 512, t, 256
    states, bufs = [], []
    ckv = kr = ckv_b = kr_b = None
    for l in range(DEPTH):
        if l < N_GLA:
            x, s = _gla_layer(x, s0, 0, gla_w[l], tm=tm_gla, cp=GLA_CHUNK, n_valid=GLA_CHUNK,
                              tiles_per_state=t // tm_gla)
            states.append(s)
        else:
            if l == N_GLA:
                ckv, kr, ckv_b, kr_b = _kv_proj(x, cos, sin, kv_w, tm=tm_kv)
            x = _mla_prompt(x, ckv_b, kr_b, cos2, sin2, mla_w[l - N_GLA], tq=tq, seq=t)
        x, b = _ffn_seq(x, buf0, ffn_w, l, tm=tm_ffn, tiles_per_seq=t // tm_ffn)
        bufs.append(b)
    return (x.reshape(bsz, t, d), jnp.stack(states), jnp.stack(bufs),
            ckv.reshape(bsz, t, -1), kr.reshape(bsz, t, -1))


def _heads_to_seq(q, bsz, t):
    h, _, n = q.shape
    return q.reshape(h, bsz, -1, n)[:, :, :t].transpose(1, 0, 2, 3).reshape(bsz, h * t, n)


def _seq_to_heads(o, tp):
    bsz, m, n = o.shape
    t = m // MLA_H
    o = o.reshape(bsz, MLA_H, t, n).transpose(1, 0, 2, 3)
    return jnp.pad(o, ((0, 0), (0, 0), (0, tp - t), (0, 0))).reshape(MLA_H, bsz * tp, n)


def _trunk_sample(x, state_gla, state_conv, cache_ckv, cache_kr, page_table, weights):
    gla_w, ffn_w, mla_w, kv_w = weights
    bsz, t, d = x.shape
    tp = -(-t // SUBLANES) * SUBLANES
    assert tp & (tp - 1) == 0 and t >= CONV_W - 1
    rows = bsz * tp
    page = cache_ckv.shape[1]
    past = page_table.shape[1] * page
    x = jnp.pad(x, ((0, 0), (0, tp - t), (0, 0))).reshape(rows, d)
    cos, sin = _rope_tables(past + jnp.arange(tp, dtype=jnp.int32))
    cos, sin = jnp.tile(cos, (bsz, 1)), jnp.tile(sin, (bsz, 1))
    cos2, sin2 = jnp.concatenate([cos, cos], -1), jnp.concatenate([sin, sin], -1)
    cache_krt = jnp.swapaxes(cache_kr, 1, 2)
    tm_gla = 8 * tp
    pad_tail = ((0, 0), (0, tp - (CONV_W - 1)), (0, 0))
    states, bufs = [], []
    ckv = kr = ckv_pad = kr_pad = None
    for l in range(DEPTH):
        if l < N_GLA:
            x, s = _gla_layer(x, state_gla, l, gla_w[l], tm=tm_gla, cp=tp, n_valid=t, tiles_per_state=0)
            states.append(s)
        else:
            w = mla_w[l - N_GLA]
            if l == N_GLA:
                ckv, kr, _, _ = _kv_proj(x, cos, sin, kv_w, tm=rows)
                ckv = ckv.reshape(bsz, tp, -1)
                kr = kr.reshape(bsz, tp, -1)
                ckv_pad = jnp.pad(ckv, ((0, 0), (0, page - tp), (0, 0)))
                kr_pad = jnp.pad(kr, ((0, 0), (0, page - tp), (0, 0)))
            qa, qr = _mla_q(x, cos2, sin2, w, tm=rows)
            ol = _mla_attend_sample(_heads_to_seq(qa, bsz, t), _heads_to_seq(qr, bsz, t), ckv_pad, kr_pad,
                                    cache_ckv, cache_krt, page_table, tq=t, pages=64)
            x = _mla_out(x, _seq_to_heads(ol, tp), w, tm=rows)
        buf = state_conv[l]
        prev1 = jnp.pad(buf[:, CONV_W - 2:], ((0, 0), (0, tp - 1), (0, 0))).reshape(rows, FFN_DIM)
        prev2 = jnp.pad(buf, pad_tail).reshape(rows, FFN_DIM)
        x, u = _ffn_rows(x, prev1, prev2, ffn_w, l, tm=rows, tp=tp)
        bufs.append(u.reshape(bsz, tp, FFN_DIM)[:, t - (CONV_W - 1):t])
    return (x.reshape(bsz, tp, d)[:, :t], jnp.stack(states), jnp.stack(bufs), ckv[:, :t], kr[:, :t])


def kernel(x_prompt, x_sample, state_gla, state_ffn_conv, cache_ckv, cache_krope, page_table, norm_mix_pre, norm_mix_post, norm_ffn_pre, norm_ffn_post, ffn_w_gate, ffn_w_up, ffn_conv_w, ffn_conv_b, ffn_w_down, gla_w_q, gla_w_k, gla_w_v, gla_w_gk1, gla_w_gk2, gla_b_gk, gla_w_r, gla_norm, gla_w_o, mla_kv_src_norm, mla_w_dkv, mla_kv_norm, mla_w_kr, mla_w_uk, mla_w_uv, mla_w_dq, mla_q_norm, mla_w_uq, mla_w_o):
    p = dict(norm_mix_pre=norm_mix_pre, norm_mix_post=norm_mix_post, norm_ffn_pre=norm_ffn_pre,
             norm_ffn_post=norm_ffn_post, ffn_w_gate=ffn_w_gate, ffn_w_up=ffn_w_up, ffn_conv_w=ffn_conv_w,
             ffn_conv_b=ffn_conv_b, ffn_w_down=ffn_w_down, gla_w_q=gla_w_q, gla_w_k=gla_w_k, gla_w_v=gla_w_v,
             gla_w_gk1=gla_w_gk1, gla_w_gk2=gla_w_gk2, gla_b_gk=gla_b_gk, gla_w_r=gla_w_r, gla_norm=gla_norm,
             gla_w_o=gla_w_o, mla_kv_src_norm=mla_kv_src_norm, mla_w_dkv=mla_w_dkv, mla_kv_norm=mla_kv_norm,
             mla_w_kr=mla_w_kr, mla_w_uk=mla_w_uk, mla_w_uv=mla_w_uv, mla_w_dq=mla_w_dq, mla_q_norm=mla_q_norm,
             mla_w_uq=mla_w_uq, mla_w_o=mla_w_o)
    weights = _prep_weights(p)
    y_p, gla_p, conv_p, ckv_p, kr_p = _trunk_prompt(x_prompt, weights)
    y_s, gla_s, conv_s, ckv_s, kr_s = _trunk_sample(x_sample, state_gla, state_ffn_conv, cache_ckv,
                                                    cache_krope, page_table, weights)
    return (y_p, y_s, gla_p, gla_s, conv_p, conv_s, ckv_p, ckv_s, kr_p, kr_s)
```

```python
import functools

import jax
import jax.numpy as jnp
from jax import lax
from jax.experimental import pallas as pl
from jax.experimental.pallas import tpu as pltpu

D_MODEL = 1024
DEPTH = 4
N_GLA = DEPTH // 2
GLA_H = 4
GLA_DK = 128
GLA_DV = 256
GLA_GATE_TEMP = 16.0
GLA_CHUNK = 64
GLA_SUBTILE = 256
MLA_H = 8
MLA_NOPE = 128
MLA_ROPE = 64
MLA_V = 128
MLA_Q_RANK = 384
MLA_KV_RANK = 256
ROPE_THETA = 10000.0
FFN_DIM = 2816
CONV_W = 3
EPS = 1e-6
LANES = 128
SUBLANES = 8
MXU_DIM = 256
VMEM_LIMIT = 56 * 1024 * 1024

F32 = jnp.float32
BF16 = jnp.bfloat16
NT_DIMS = (((1,), (1,)), ((), ()))
TN_DIMS = (((0,), (0,)), ((), ()))
LOG2_E = 1.4426950408889634
SM_SCALE_LOG2 = (MLA_NOPE + MLA_ROPE) ** -0.5 * LOG2_E


def _rms(x, g):
    return x * lax.rsqrt(jnp.mean(x * x, axis=-1, keepdims=True) + EPS) * g


def _dot(a, b):
    return jnp.dot(a, b, preferred_element_type=F32)


def _dot_nt(a, b):
    return lax.dot_general(a, b, NT_DIMS, preferred_element_type=F32)


def _log_sigmoid(z):
    return jnp.minimum(z, 0.0) - jnp.log1p(jnp.exp(-jnp.abs(z)))


def _gelu_tanh(x):
    cdf = 0.5 * (1.0 + jnp.tanh(0.7978845608028654 * (x + 0.044715 * (x * x * x))))
    return x * cdf


def _const_spec(shape):
    nd = len(shape)
    return pl.BlockSpec(shape, lambda *_: (0,) * nd, pipeline_mode=pl.Buffered(1))


def _params(n_axes):
    return pltpu.CompilerParams(dimension_semantics=("arbitrary",) * n_axes,
                                vmem_limit_bytes=VMEM_LIMIT)


def _gla_kernel(x_ref, s0_ref, gpre_ref, wqkvr_ref, wg1_ref, wg2_ref, bgk_ref, gn_ref, wo_ref,
                gpost_ref, y_ref, s_ref, proj_ref, lg_ref, o_ref, *, tm, cp, n_valid, tiles_per_state):
    if tiles_per_state:
        @pl.when(pl.program_id(0) % tiles_per_state == 0)
        def _():
            s_ref[...] = s0_ref[...]
    else:
        s_ref[...] = s0_ref[...]

    x = x_ref[...]
    h = _rms(x, gpre_ref[...]).astype(BF16)
    g1 = _dot(h, wg1_ref[...])
    z = _dot(g1.astype(BF16), wg2_ref[...]) + bgk_ref[...]
    lg = _log_sigmoid(z) / GLA_GATE_TEMP
    if n_valid < cp:
        t_in_chunk = lax.broadcasted_iota(jnp.int32, (tm, 1), 0) & (cp - 1)
        lg = jnp.where(t_in_chunk < n_valid, lg, 0.0)
    lg_ref[...] = lg
    proj_ref[...] = _dot(h, wqkvr_ref[...])

    ts = min(tm, GLA_SUBTILE)
    nc = ts // cp
    r_i = lax.broadcasted_iota(jnp.int32, (ts, ts), 0)
    c_i = lax.broadcasted_iota(jnp.int32, (ts, ts), 1)
    intra = (r_i >= c_i) & ((r_i & -cp) == (c_i & -cp))
    t_in = lax.broadcasted_iota(jnp.int32, (ts, GLA_DK), 0) & (cp - 1)
    eye =(lax.broadcasted_iota(jnp.int32, (GLA_DK, GLA_DK), 0)
           == lax.broadcasted_iota(jnp.int32, (GLA_DK, GLA_DK), 1))
    k_off = GLA_H * GLA_DK
    v_off = 2 * GLA_H * GLA_DK
    r_off = v_off + GLA_H * GLA_DV

    for hd in range(GLA_H):
        kcol = slice(hd * GLA_DK, (hd + 1) * GLA_DK)
        s = s_ref[0, hd]
        for sub in range(tm // ts):
            r0 = sub * ts
            b = lg_ref[r0:r0 + ts, kcol]
            shift = 1
            while shift < cp:
                b = b + jnp.where(t_in >= shift, pltpu.roll(b, shift, 0), 0.0)
                shift *= 2
            bl = jnp.concatenate(
                [jnp.broadcast_to(b[(c + 1) * cp - 1:(c + 1) * cp, :], (cp, GLA_DK)) for c in range(nc)], axis=0)
            q = proj_ref[r0:r0 + ts, hd * GLA_DK:(hd + 1) * GLA_DK] * (GLA_DK ** -0.5)
            k = proj_ref[r0:r0 + ts, k_off + hd * GLA_DK:k_off + (hd + 1) * GLA_DK]
            v = proj_ref[r0:r0 + ts, v_off + hd * GLA_DV:v_off + (hd + 1) * GLA_DV]
            qe = q * jnp.exp(b)
            ke = k * jnp.exp(-b)
            kd = k * jnp.exp(bl - b)
            a = jnp.where(intra, _dot_nt(qe.astype(BF16), ke.astype(BF16)), 0.0).astype(BF16)
            o_intra = _dot(a, v.astype(BF16))
            for c in range(nc):
                rows = slice(c * cp, (c + 1) * cp)
                si = sub * nc + c
                if not tiles_per_state:
                    s = s_ref[si, hd]
                o_ref[r0 + c * cp:r0 + (c + 1) * cp, hd * GLA_DV:(hd + 1) * GLA_DV] = (
                    o_intra[rows, :] + _dot(qe[rows, :].astype(BF16), s.astype(BF16)))
                dcol = jnp.sum(jnp.where(eye, jnp.exp(bl[c * cp:c * cp + 1, :]), 0.0), axis=1, keepdims=True)
                s = dcol * s + lax.dot_general(kd[rows, :].astype(BF16), v[rows, :].astype(BF16), TN_DIMS,
                                               preferred_element_type=F32)
                if not tiles_per_state:
                    s_ref[si, hd] = s
        if tiles_per_state:
            s_ref[0, hd] = s

    gn = gn_ref[...]
    for hd in range(GLA_H):
        vcol = slice(hd * GLA_DV, (hd + 1) * GLA_DV)
        r = proj_ref[:, r_off + hd * GLA_DV:r_off + (hd + 1) * GLA_DV]
        o_ref[:, vcol] = _rms(o_ref[:, vcol], gn) * (r * jax.nn.sigmoid(r))
    mix = _dot(o_ref[...].astype(BF16), wo_ref[...])
    y_ref[...] = x + _rms(mix, gpost_ref[...])


def _gla_layer(x, s0_all, layer, w, *, tm, cp, n_valid, tiles_per_state):
    rows, d = x.shape
    assert rows % tm == 0 and tm % cp == 0 and cp & (cp - 1) == 0
    n_proj = w["wqkvr"].shape[1]
    x_spec = pl.BlockSpec((tm, d), lambda i: (i, 0))
    n_blk = 1 if tiles_per_state else tm // cp
    per = tiles_per_state or 1
    s_in_spec = pl.BlockSpec((None, n_blk, GLA_H, GLA_DK, GLA_DV), lambda i: (layer, i // per, 0, 0, 0))
    s_spec = pl.BlockSpec((n_blk, GLA_H, GLA_DK, GLA_DV), lambda i: (i // per, 0, 0, 0))
    return pl.pallas_call(
        functools.partial(_gla_kernel, tm=tm, cp=cp, n_valid=n_valid, tiles_per_state=tiles_per_state),
        grid=(rows // tm,),
        in_specs=[x_spec, s_in_spec, _const_spec((1, d)), _const_spec(w["wqkvr"].shape),
                  _const_spec(w["wg1"].shape), _const_spec(w["wg2"].shape), _const_spec(w["bgk"].shape),
                  _const_spec(w["gnorm"].shape), _const_spec(w["wo"].shape), _const_spec((1, d))],
        out_specs=[x_spec, s_spec],
        out_shape=[jax.ShapeDtypeStruct(x.shape, F32), jax.ShapeDtypeStruct(s0_all.shape[1:], F32)],
        scratch_shapes=[pltpu.VMEM((tm, n_proj), F32), pltpu.VMEM((tm, GLA_H * GLA_DK), F32),
                        pltpu.VMEM((tm, GLA_H * GLA_DV), F32)],
        compiler_params=_params(1),
        name="gla_layer",
    )(x, s0_all, w["gpre"], w["wqkvr"], w["wg1"], w["wg2"], w["bgk"], w["gnorm"], w["wo"], w["gpost"])


def _ffn_core(x, gpre_ref, wg_ref, wu_ref, cw_ref, cb_ref, wd_ref, gpost_ref, shifted, emit_u):
    h = _rms(x, gpre_ref[...]).astype(BF16)
    u = _dot(h, wg_ref[...])
    up = _dot(h, wu_ref[...])
    u1, u2 = shifted(u)
    c = cb_ref[...] + ((cw_ref[0:1, :] * u2 + cw_ref[1:2, :] * u1) + cw_ref[2:3, :] * u)
    act = (_gelu_tanh(c) * up).astype(BF16)
    emit_u(u)
    return x + _rms(_dot(act, wd_ref[...]), gpost_ref[...])


def _ffn_seq_kernel(x_ref, buf_ref, gpre_ref, wg_ref, wu_ref, cw_ref, cb_ref, wd_ref, gpost_ref,
                    y_ref, bufn_ref, carry_ref, *, tm, tiles_per_seq):
    @pl.when(pl.program_id(0) % tiles_per_seq == 0)
    def _():
        carry_ref[...] = buf_ref[0]

    row = lax.broadcasted_iota(jnp.int32, (tm, 1), 0)

    def shifted(u):
        p0 = carry_ref[0:1, :]
        p1 = carry_ref[1:2, :]
        u1 = jnp.where(row == 0, p1, pltpu.roll(u, 1, 0))
        u2 = jnp.where(row == 0, p0, jnp.where(row == 1, p1, pltpu.roll(u, 2, 0)))
        return u1, u2

    def emit_u(u):
        tail = u[tm - (CONV_W - 1):tm, :]
        carry_ref[...] = tail
        bufn_ref[0] = tail

    y_ref[...] = _ffn_core(x_ref[...], gpre_ref, wg_ref, wu_ref, cw_ref, cb_ref, wd_ref, gpost_ref,
                           shifted, emit_u)


def _ffn_rows_kernel(x_ref, prev1_ref, prev2_ref, gpre_ref, wg_ref, wu_ref, cw_ref, cb_ref, wd_ref,
                     gpost_ref, y_ref, u_ref, *, tm, tp):
    t = lax.broadcasted_iota(jnp.int32, (tm, 1), 0) & (tp - 1)

    def shifted(u):
        u1 = jnp.where(t == 0, prev1_ref[...], pltpu.roll(u, 1, 0))
        u2 = jnp.where(t < 2, prev2_ref[...], pltpu.roll(u, 2, 0))
        return u1, u2

    def emit_u(u):
        u_ref[...] = u

    y_ref[...] = _ffn_core(x_ref[...], gpre_ref, wg_ref, wu_ref, cw_ref, cb_ref, wd_ref, gpost_ref,
                           shifted, emit_u)


FFN_KEYS = ("gpre", "wg", "wu", "cw", "cb", "wd", "gpost")


def _ffn_weight_specs(w, layer):
    return [pl.BlockSpec((None,) + w[k].shape[1:], lambda *_: (layer, 0, 0), pipeline_mode=pl.Buffered(1))
            for k in FFN_KEYS]


def _ffn_weights(w):
    return [w[k] for k in FFN_KEYS]


def _ffn_seq(x, buf, w, layer, *, tm, tiles_per_seq):
    rows, d = x.shape
    assert rows % (tm * tiles_per_seq) == 0 and tm >= CONV_W - 1
    x_spec = pl.BlockSpec((tm, d), lambda i: (i, 0))
    b_spec = pl.BlockSpec((1, CONV_W - 1, FFN_DIM), lambda i: (i // tiles_per_seq, 0, 0))
    return pl.pallas_call(
        functools.partial(_ffn_seq_kernel, tm=tm, tiles_per_seq=tiles_per_seq),
        grid=(rows // tm,),
        in_specs=[x_spec, b_spec] + _ffn_weight_specs(w, layer),
        out_specs=[x_spec, b_spec],
        out_shape=[jax.ShapeDtypeStruct(x.shape, F32), jax.ShapeDtypeStruct(buf.shape, F32)],
        scratch_shapes=[pltpu.VMEM((CONV_W - 1, FFN_DIM), F32)],
        compiler_params=_params(1),
        name="ffn_seq",
    )(x, buf, *_ffn_weights(w))


def _ffn_rows(x, prev1, prev2, w, layer, *, tm, tp):
    rows, d = x.shape
    assert rows % tm == 0 and tm % tp == 0 and tp & (tp - 1) == 0
    x_spec = pl.BlockSpec((tm, d), lambda i: (i, 0))
    u_spec = pl.BlockSpec((tm, FFN_DIM), lambda i: (i, 0))
    return pl.pallas_call(
        functools.partial(_ffn_rows_kernel, tm=tm, tp=tp),
        grid=(rows // tm,),
        in_specs=[x_spec, u_spec, u_spec] + _ffn_weight_specs(w, layer),
        out_specs=[x_spec, u_spec],
        out_shape=[jax.ShapeDtypeStruct(x.shape, F32), jax.ShapeDtypeStruct((rows, FFN_DIM), F32)],
        compiler_params=_params(1),
        name="ffn_rows",
    )(x, prev1, prev2, *_ffn_weights(w))


def _kv_math(x, cos_ref, sin_ref, gsrc_ref, wdkv_ref, gkv_ref, wkr_ref, wkrot_ref,
             ckv_ref, kr_ref, ckvb_ref, krb_ref):
    hk = _rms(x, gsrc_ref[...]).astype(BF16)
    ckv = _rms(_dot(hk, wdkv_ref[...]), gkv_ref[...])
    kr = _dot(hk, wkr_ref[...]) * cos_ref[...] + _dot(hk, wkrot_ref[...]) * sin_ref[...]
    ckv_ref[...] = ckv
    kr_ref[...] = kr
    ckvb_ref[...] = ckv.astype(BF16)
    krb_ref[...] = kr.astype(BF16)


def _kv_kernel(x_ref, *refs):
    _kv_math(x_ref[...], *refs)


KV_KEYS = ("gsrc", "wdkv", "gkv", "wkr", "wkrot")


def _kv_specs(rows, d, period, tm):
    t_spec = pl.BlockSpec((tm, MLA_ROPE), lambda i: (i % (period // tm), 0))
    c_spec = pl.BlockSpec((tm, MLA_KV_RANK), lambda i: (i, 0))
    r_spec = pl.BlockSpec((tm, MLA_ROPE), lambda i: (i, 0))
    in_specs = [t_spec, t_spec, _const_spec((1, d)), _const_spec((d, MLA_KV_RANK)),
                _const_spec((1, MLA_KV_RANK)), _const_spec((d, MLA_ROPE)), _const_spec((d, MLA_ROPE))]
    out_shape = [jax.ShapeDtypeStruct((rows, MLA_KV_RANK), F32), jax.ShapeDtypeStruct((rows, MLA_ROPE), F32),
                 jax.ShapeDtypeStruct((rows, MLA_KV_RANK), BF16), jax.ShapeDtypeStruct((rows, MLA_ROPE), BF16)]
    return in_specs, [c_spec, r_spec, c_spec, r_spec], out_shape


def _kv_proj(x, cos, sin, w, *, tm):
    rows, d = x.shape
    period = cos.shape[0]
    assert rows % tm == 0 and period % tm == 0
    in_specs, out_specs, out_shape = _kv_specs(rows, d, period, tm)
    return pl.pallas_call(
        _kv_kernel,
        grid=(rows // tm,),
        in_specs=[pl.BlockSpec((tm, d), lambda i: (i, 0))] + in_specs,
        out_specs=out_specs,
        out_shape=out_shape,
        compiler_params=_params(1),
        name="kv_proj",
    )(x, cos, sin, *[w[k] for k in KV_KEYS])


def _mla_q_heads(x, cos, sin, gpre_ref, wdq_ref, qn_ref, wuq_ref, wuk_ref):
    h = _rms(x, gpre_ref[...]).astype(BF16)
    cq = _rms(_dot(h, wdq_ref[...]), qn_ref[...]).astype(BF16)
    q = _dot(cq, wuq_ref[...])
    hw = MLA_H * LANES
    for hd in range(MLA_H):
        qn = q[:, hd * MLA_NOPE:(hd + 1) * MLA_NOPE].astype(BF16)
        qa = _dot_nt(qn, wuk_ref[hd])
        roped = (q[:, hw + hd * LANES:hw + (hd + 1) * LANES] * cos
                 + q[:, 2 * hw + hd * LANES:2 * hw + (hd + 1) * LANES] * sin)
        yield hd, qa, roped[:, :MLA_ROPE]


def _lane_fold(x, op):
    parts = [x[:, c:c + LANES] for c in range(0, x.shape[1], LANES)]
    while len(parts) > 1:
        parts = [op(parts[i], parts[i + 1]) for i in range(0, len(parts) - 1, 2)] + parts[len(parts) & ~1:]
    return parts[0]


def _lane_tile(x, width):
    return x if width == LANES else jnp.concatenate([x] * (width // LANES), axis=1)


def _softmax_step(s, pv_fn, m_ref, l_ref, acc_ref, rows):
    m_prev = m_ref[rows, :]
    m_new = jnp.maximum(m_prev, jnp.max(_lane_fold(s, jnp.maximum), axis=1, keepdims=True))
    alpha = jnp.exp2(m_prev - m_new)
    p = jnp.exp2(s - _lane_tile(m_new, s.shape[1]))
    l_ref[rows, :] = alpha * l_ref[rows, :] + _lane_fold(p, jnp.add)
    acc_ref[rows, :] = _lane_tile(alpha, acc_ref.shape[1]) * acc_ref[rows, :] + pv_fn(p.astype(BF16))
    m_ref[rows, :] = m_new


def _softmax_result(l_ref, acc_ref, rows):
    return acc_ref[rows, :] / jnp.sum(l_ref[rows, :], axis=1, keepdims=True)


def _mla_out_heads(x, ol_fn, oc_ref, wuv_ref, wo_ref, gpost_ref):
    for hd in range(MLA_H):
        oc_ref[:, hd * MLA_V:(hd + 1) * MLA_V] = _dot(ol_fn(hd).astype(BF16), wuv_ref[hd])
    mix = _dot(oc_ref[...].astype(BF16), wo_ref[...])
    return x + _rms(mix, gpost_ref[...])


def _mla_prompt_kernel(x_ref, ckv_ref, kr_ref, cos_ref, sin_ref, gpre_ref, wdq_ref, qn_ref, wuq_ref,
                       wuk_ref, wuv_ref, wo_ref, gpost_ref, y_ref,
                       qa_ref, qr_ref, m_ref, l_ref, acc_ref, oc_ref, *, tq):
    i = pl.program_id(1)
    x = x_ref[...]
    for hd, qa, qr in _mla_q_heads(x, cos_ref[...], sin_ref[...], gpre_ref, wdq_ref, qn_ref, wuq_ref, wuk_ref):
        qa_ref[hd * tq:(hd + 1) * tq, :] = (qa * SM_SCALE_LOG2).astype(BF16)
        qr_ref[hd * tq:(hd + 1) * tq, :] = (qr * SM_SCALE_LOG2).astype(BF16)
    m_ref[...] = jnp.full(m_ref.shape, -jnp.inf, F32)
    l_ref[...] = jnp.zeros(l_ref.shape, F32)
    acc_ref[...] = jnp.zeros(acc_ref.shape, F32)

    every = slice(None)

    def key_block(j, causal):
        k0 = pl.multiple_of(j * tq, tq)
        kc = ckv_ref[pl.ds(k0, tq), :]
        kk = kr_ref[pl.ds(k0, tq), :]
        s = _dot_nt(qa_ref[...], kc) + _dot_nt(qr_ref[...], kk)
        if causal:
            visible = (lax.broadcasted_iota(jnp.int32, s.shape, 1)
                       <= (lax.broadcasted_iota(jnp.int32, s.shape, 0) & (tq - 1)))
            s = jnp.where(visible, s, -jnp.inf)
        _softmax_step(s, lambda p: _dot(p, kc), m_ref, l_ref, acc_ref, every)

    def body(j, carry):
        key_block(j, False)
        return carry

    lax.fori_loop(0, i, body, 0)
    key_block(i, True)

    def ol_fn(hd):
        return _softmax_result(l_ref, acc_ref, slice(hd * tq, (hd + 1) * tq))

    y_ref[...] = _mla_out_heads(x, ol_fn, oc_ref, wuv_ref, wo_ref, gpost_ref)


def _mla_weight_specs(w, keys):
    return [_const_spec(w[k].shape) for k in keys]


MLA_Q_KEYS = ("gpre", "wdq", "qnorm", "wuq", "wuk")
MLA_O_KEYS = ("wuv", "wo", "gpost")


def _mla_prompt(x, ckv_b, kr_b, cos, sin, w, *, tq, seq):
    rows, d = x.shape
    nt = seq // tq
    assert rows % seq == 0 and seq % tq == 0
    m = MLA_H * tq
    x_spec = pl.BlockSpec((tq, d), lambda b, i: (b * nt + i, 0))
    t_spec = pl.BlockSpec((tq, LANES), lambda b, i: (i, 0))
    keys = MLA_Q_KEYS + MLA_O_KEYS
    return pl.pallas_call(
        functools.partial(_mla_prompt_kernel, tq=tq),
        grid=(rows // seq, nt),
        in_specs=[x_spec,
                  pl.BlockSpec((seq, MLA_KV_RANK), lambda b, i: (b, 0)),
                  pl.BlockSpec((seq, MLA_ROPE), lambda b, i: (b, 0)),
                  t_spec, t_spec] + _mla_weight_specs(w, keys),
        out_specs=x_spec,
        out_shape=jax.ShapeDtypeStruct(x.shape, F32),
        scratch_shapes=[pltpu.VMEM((m, MLA_KV_RANK), BF16), pltpu.VMEM((m, MLA_ROPE), BF16),
                        pltpu.VMEM((m, LANES), F32), pltpu.VMEM((m, LANES), F32), pltpu.VMEM((m, MLA_KV_RANK), F32),
                        pltpu.VMEM((tq, MLA_H * MLA_V), F32)],
        compiler_params=_params(2),
        name="mla_prompt",
    )(x, ckv_b, kr_b, cos, sin, *[w[k] for k in keys])


def _mla_q_kernel(x_ref, cos_ref, sin_ref, gpre_ref, wdq_ref, qn_ref, wuq_ref, wuk_ref, qa_ref, qr_ref):
    for hd, qa, qr in _mla_q_heads(x_ref[...], cos_ref[...], sin_ref[...], gpre_ref, wdq_ref, qn_ref,
                                   wuq_ref, wuk_ref):
        qa_ref[hd] = qa
        qr_ref[hd] = qr


def _mla_q(x, cos, sin, w, *, tm):
    rows, d = x.shape
    assert rows % tm == 0
    return pl.pallas_call(
        _mla_q_kernel,
        grid=(rows // tm,),
        in_specs=[pl.BlockSpec((tm, d), lambda i: (i, 0)), pl.BlockSpec((tm, LANES), lambda i: (i, 0)),
                  pl.BlockSpec((tm, LANES), lambda i: (i, 0))] + _mla_weight_specs(w, MLA_Q_KEYS),
        out_specs=[pl.BlockSpec((MLA_H, tm, MLA_KV_RANK), lambda i: (0, i, 0)),
                   pl.BlockSpec((MLA_H, tm, MLA_ROPE), lambda i: (0, i, 0))],
        out_shape=[jax.ShapeDtypeStruct((MLA_H, rows, MLA_KV_RANK), F32),
                   jax.ShapeDtypeStruct((MLA_H, rows, MLA_ROPE), F32)],
        compiler_params=_params(1),
        name="mla_q",
    )(x, cos, sin, *[w[k] for k in MLA_Q_KEYS])


def _mla_attend_sample_kernel(pt_ref, qa_ref, qr_ref, ckvn_ref, krn_ref, ckv_hbm, krt_hbm, ol_ref,
                              ckv_buf, krt_buf, sem, m_ref, l_ref, acc_ref, *, tq, pages):
    b = pl.program_id(0)
    j = pl.program_id(1)
    nj = pl.num_programs(1)
    last = pl.num_programs(0) * nj - 1
    step = b * nj + j
    ahead = PAGE_SLOTS - 1
    slot = lax.rem(step, PAGE_SLOTS)
    every = slice(None)
    n_streams = m_ref.shape[0]
    per_stream = pages // n_streams

    def page_copies(n):
        sl = lax.rem(n, PAGE_SLOTS)
        src = jnp.minimum(n, last)
        bb = src // nj
        jj = src - bb * nj
        out = []
        for i in range(pages):
            pid = pt_ref[bb, jj * pages + i]
            out.append(pltpu.make_async_copy(ckv_hbm.at[pid], ckv_buf.at[sl, i], sem.at[sl]))
            out.append(pltpu.make_async_copy(krt_hbm.at[pid], krt_buf.at[sl, i], sem.at[sl]))
        return out

    @pl.when(step == 0)
    def _():
        for n in range(ahead):
            for c in page_copies(n):
                c.start()

    @pl.when(j == 0)
    def _():
        m_ref[...] = jnp.full(m_ref.shape, -jnp.inf, F32)
        l_ref[...] = jnp.zeros(l_ref.shape, F32)
        acc_ref[...] = jnp.zeros(acc_ref.shape, F32)

    qa = (qa_ref[0] * SM_SCALE_LOG2).astype(BF16)
    qr = (qr_ref[0] * SM_SCALE_LOG2).astype(BF16)
    for c in page_copies(step):
        c.wait()
    for st in range(n_streams):
        ids = range(st * per_stream, (st + 1) * per_stream)
        kcs = [ckv_buf[slot, i].astype(BF16) for i in ids]
        s = jnp.concatenate([_dot_nt(qa, kc) + _dot(qr, krt_buf[slot, i].astype(BF16)) for i, kc in zip(ids, kcs)],
                            axis=1)
        page = kcs[0].shape[0]

        def pv_pages(p, kcs=kcs):
            out = _dot(p[:, :page], kcs[0])
            for i in range(1, len(kcs)):
                out = out + _dot(p[:, i * page:(i + 1) * page], kcs[i])
            return out

        _softmax_step(s, pv_pages, m_ref.at[st], l_ref.at[st], acc_ref.at[st], every)

    for c in page_copies(step + ahead):
        c.start()

    @pl.when(step == last)
    def _():
        for n in range(1, ahead + 1):
            for c in page_copies(last + n):
                c.wait()

    @pl.when(j == nj - 1)
    def _():
        kc = ckvn_ref[0].astype(BF16)
        kk = krn_ref[0].astype(BF16)
        n_new = kc.shape[0]
        qpos = lax.broadcasted_iota(jnp.int32, (qa.shape[0], 1), 0) & (tq - 1)
        kpos = lax.broadcasted_iota(jnp.int32, (1, n_new), 1)
        sn = jnp.where(kpos <= qpos, _dot_nt(qa, kc) + _dot_nt(qr, kk), -jnp.inf)
        _softmax_step(sn, lambda p: _dot(p, kc), m_ref.at[0], l_ref.at[0], acc_ref.at[0], every)
        m_all = m_ref[0]
        for st in range(1, n_streams):
            m_all = jnp.maximum(m_all, m_ref[st])
        l_all = jnp.zeros(l_ref.shape[1:], F32)
        acc_all = jnp.zeros(acc_ref.shape[1:], F32)
        for st in range(n_streams):
            w_st = jnp.exp2(m_ref[st] - m_all)
            l_all = l_all + w_st * l_ref[st]
            acc_all = acc_all + _lane_tile(w_st, acc_all.shape[1]) * acc_ref[st]
        ol_ref[0] = acc_all / jnp.sum(l_all, axis=1, keepdims=True)


SAMPLE_SOFTMAX_STREAMS = 2
PAGE_SLOTS = 3


def _mla_attend_sample(qa, qr, ckv_new, kr_new, cache_ckv, cache_krt, page_table, *, tq, pages):
    bsz, n_pages = page_table.shape
    page = cache_ckv.shape[1]
    assert n_pages % pages == 0 and pages % SAMPLE_SOFTMAX_STREAMS == 0
    assert tq & (tq - 1) == 0 and ckv_new.shape[1] == page
    m = MLA_H * tq
    q_spec = pl.BlockSpec((1, m, MLA_KV_RANK), lambda b, j, pt: (b, 0, 0))
    r_spec = pl.BlockSpec((1, m, MLA_ROPE), lambda b, j, pt: (b, 0, 0))
    hbm_spec = pl.BlockSpec(memory_space=pl.ANY)
    grid_spec = pltpu.PrefetchScalarGridSpec(
        num_scalar_prefetch=1,
        grid=(bsz, n_pages // pages),
        in_specs=[q_spec, r_spec,
                  pl.BlockSpec((1, page, MLA_KV_RANK), lambda b, j, pt: (b, 0, 0)),
                  pl.BlockSpec((1, page, MLA_ROPE), lambda b, j, pt: (b, 0, 0)),
                  hbm_spec, hbm_spec],
        out_specs=q_spec,
        scratch_shapes=[pltpu.VMEM((PAGE_SLOTS, pages, page, MLA_KV_RANK), cache_ckv.dtype),
                        pltpu.VMEM((PAGE_SLOTS, pages, MLA_ROPE, page), cache_krt.dtype),
                        pltpu.SemaphoreType.DMA((PAGE_SLOTS,)),
                        pltpu.VMEM((SAMPLE_SOFTMAX_STREAMS, m, LANES), F32),
                        pltpu.VMEM((SAMPLE_SOFTMAX_STREAMS, m, LANES), F32),
                        pltpu.VMEM((SAMPLE_SOFTMAX_STREAMS, m, MLA_KV_RANK), F32)],
    )
    return pl.pallas_call(
        functools.partial(_mla_attend_sample_kernel, tq=tq, pages=pages),
        grid_spec=grid_spec,
        out_shape=jax.ShapeDtypeStruct(qa.shape, F32),
        compiler_params=_params(2),
        name="mla_attend_sample",
    )(page_table, qa, qr, ckv_new, kr_new, cache_ckv, cache_krt)


def _mla_out_kernel(x_ref, ol_ref, wuv_ref, wo_ref, gpost_ref, y_ref, oc_ref):
    y_ref[...] = _mla_out_heads(x_ref[...], lambda hd: ol_ref[hd], oc_ref, wuv_ref, wo_ref, gpost_ref)


def _mla_out(x, ol, w, *, tm):
    rows, d = x.shape
    assert rows % tm == 0
    x_spec = pl.BlockSpec((tm, d), lambda i: (i, 0))
    return pl.pallas_call(
        _mla_out_kernel,
        grid=(rows // tm,),
        in_specs=[x_spec, pl.BlockSpec((MLA_H, tm, MLA_KV_RANK), lambda i: (0, i, 0))]
        + _mla_weight_specs(w, MLA_O_KEYS),
        out_specs=x_spec,
        out_shape=jax.ShapeDtypeStruct(x.shape, F32),
        scratch_shapes=[pltpu.VMEM((tm, MLA_H * MLA_V), F32)],
        compiler_params=_params(1),
        name="mla_out",
    )(x, ol, *[w[k] for k in MLA_O_KEYS])


def _row(v):
    return v.reshape(1, -1).astype(F32)


def _rot_cols(w):
    half = w.shape[-1] // 2
    return jnp.concatenate([-w[..., half:], w[..., :half]], axis=-1)


def _prep_weights(p):
    gla, mla = [], []
    for l in range(N_GLA):
        rank = p["gla_w_gk1"].shape[-1]
        gla.append(dict(
            gpre=_row(p["norm_mix_pre"][l]), gpost=_row(p["norm_mix_post"][l]),
            wqkvr=jnp.concatenate([p["gla_w_q"][l], p["gla_w_k"][l], p["gla_w_v"][l], p["gla_w_r"][l]],
                                  axis=1).astype(BF16),
            wg1=jnp.pad(p["gla_w_gk1"][l], ((0, 0), (0, LANES - rank))).astype(BF16),
            wg2=jnp.pad(p["gla_w_gk2"][l], ((0, LANES - rank), (0, 0))).astype(BF16),
            bgk=_row(p["gla_b_gk"][l]), gnorm=_row(p["gla_norm"][l]), wo=p["gla_w_o"][l].astype(BF16)))
    ffn = dict(
        gpre=p["norm_ffn_pre"][:, None, :].astype(F32), gpost=p["norm_ffn_post"][:, None, :].astype(F32),
        wg=p["ffn_w_gate"].astype(BF16), wu=p["ffn_w_up"].astype(BF16),
        cw=p["ffn_conv_w"].astype(F32), cb=p["ffn_conv_b"][:, None, :].astype(F32),
        wd=p["ffn_w_down"].astype(BF16))
    wuk = jnp.transpose(p["mla_w_uk"], (1, 0, 2)).astype(BF16)
    wuv = jnp.transpose(p["mla_w_uv"], (1, 0, 2)).astype(BF16)
    for j in range(DEPTH - N_GLA):
        l = N_GLA + j
        wuq = p["mla_w_uq"][j].reshape(MLA_Q_RANK, MLA_H, MLA_NOPE + MLA_ROPE)
        nope = wuq[:, :, :MLA_NOPE].reshape(MLA_Q_RANK, MLA_H * MLA_NOPE)
        rope = wuq[:, :, MLA_NOPE:]
        lane_pad = ((0, 0), (0, 0), (0, LANES - MLA_ROPE))
        rope_p = jnp.pad(rope, lane_pad).reshape(MLA_Q_RANK, MLA_H * LANES)
        rot_p = jnp.pad(_rot_cols(rope), lane_pad).reshape(MLA_Q_RANK, MLA_H * LANES)
        mla.append(dict(
            gpre=_row(p["norm_mix_pre"][l]), gpost=_row(p["norm_mix_post"][l]),
            wdq=p["mla_w_dq"][j].astype(BF16), qnorm=_row(p["mla_q_norm"][j]),
            wuq=jnp.concatenate([nope, rope_p, rot_p], axis=1).astype(BF16),
            wuk=wuk, wuv=wuv, wo=p["mla_w_o"][j].astype(BF16)))
    kv = dict(gsrc=_row(p["mla_kv_src_norm"]), wdkv=p["mla_w_dkv"].astype(BF16), gkv=_row(p["mla_kv_norm"]),
              wkr=p["mla_w_kr"].astype(BF16), wkrot=_rot_cols(p["mla_w_kr"]).astype(BF16))
    return gla, ffn, mla, kv


def _rope_tables(pos):
    half = MLA_ROPE // 2
    inv = ROPE_THETA ** (-jnp.arange(half, dtype=F32) / half)
    ang = pos.astype(F32)[:, None] * inv[None, :]
    cos, sin = jnp.cos(ang), jnp.sin(ang)
    return jnp.concatenate([cos, cos], axis=-1), jnp.concatenate([sin, sin], axis=-1)


def _trunk_prompt(x, weights):
    gla_w, ffn_w, mla_w, kv_w = weights
    bsz, t, d = x.shape
    x = x.reshape(bsz * t, d)
    cos, sin = _rope_tables(jnp.arange(t, dtype=jnp.int32))
    cos2, sin2 = jnp.concatenate([cos, cos], -1), jnp.concatenate([sin, sin], -1)
    s0 = jnp.zeros((1, bsz, GLA_H, GLA_DK, GLA_DV), F32)
    buf0 = jnp.zeros((bsz, CONV_W - 1, FFN_DIM), F32)
    tm_gla, tm_ffn, tm_kv, tq = 1024, 1024, t, 256
    states, bufs = [], []
    ckv = kr = ckv_b = kr_b = None
    for l in range(DEPTH):
        if l < N_GLA:
            x, s = _gla_layer(x, s0, 0, gla_w[l], tm=tm_gla, cp=GLA_CHUNK, n_valid=GLA_CHUNK,
                              tiles_per_state=t // tm_gla)
            states.append(s)
        else:
            if l == N_GLA:
                ckv, kr, ckv_b, kr_b = _kv_proj(x, cos, sin, kv_w, tm=tm_kv)
            x = _mla_prompt(x, ckv_b, kr_b, cos2, sin2, mla_w[l - N_GLA], tq=tq, seq=t)
        x, b = _ffn_seq(x, buf0, ffn_w, l, tm=tm_ffn, tiles_per_seq=t // tm_ffn)
        bufs.append(b)
    return (x.reshape(bsz, t, d), jnp.stack(states), jnp.stack(bufs),
            ckv.reshape(bsz, t, -1), kr.reshape(bsz, t, -1))


def _heads_to_seq(q, bsz, t):
    h, _, n = q.shape
    return q.reshape(h, bsz, -1, n)[:, :, :t].transpose(1, 0, 2, 3).reshape(bsz, h * t, n)


def _seq_to_heads(o, tp):
    bsz, m, n = o.shape
    t = m // MLA_H
    o = o.reshape(bsz, MLA_H, t, n).transpose(1, 0, 2, 3)
    return jnp.pad(o, ((0, 0), (0, 0), (0, tp - t), (0, 0))).reshape(MLA_H, bsz * tp, n)


def _trunk_sample(x, state_gla, state_conv, cache_ckv, cache_kr, page_table, weights):
    gla_w, ffn_w, mla_w, kv_w = weights
    bsz, t, d = x.shape
    tp = -(-t // SUBLANES) * SUBLANES
    assert tp & (tp - 1) == 0 and t >= CONV_W - 1
    rows = bsz * tp
    page = cache_ckv.shape[1]
    past = page_table.shape[1] * page
    x = jnp.pad(x, ((0, 0), (0, tp - t), (0, 0))).reshape(rows, d)
    cos, sin = _rope_tables(past + jnp.arange(tp, dtype=jnp.int32))
    cos, sin = jnp.tile(cos, (bsz, 1)), jnp.tile(sin, (bsz, 1))
    cos2, sin2 = jnp.concatenate([cos, cos], -1), jnp.concatenate([sin, sin], -1)
    cache_krt = jnp.swapaxes(cache_kr, 1, 2)
    tm_gla = 8 * tp
    pad_tail = ((0, 0), (0, tp - (CONV_W - 1)), (0, 0))
    states, bufs = [], []
    ckv = kr = ckv_pad = kr_pad = None
    for l in range(DEPTH):
        if l < N_GLA:
            x, s = _gla_layer(x, state_gla, l, gla_w[l], tm=tm_gla, cp=tp, n_valid=t, tiles_per_state=0)
            states.append(s)
        else:
            w = mla_w[l - N_GLA]
            if l == N_GLA:
                ckv, kr, _, _ = _kv_proj(x, cos, sin, kv_w, tm=rows)
                ckv = ckv.reshape(bsz, tp, -1)
                kr = kr.reshape(bsz, tp, -1)
                ckv_pad = jnp.pad(ckv, ((0, 0), (0, page - tp), (0, 0)))
                kr_pad = jnp.pad(kr, ((0, 0), (0, page - tp), (0, 0)))
            qa, qr = _mla_q(x, cos2, sin2, w, tm=rows)
            ol = _mla_attend_sample(_heads_to_seq(qa, bsz, t), _heads_to_seq(qr, bsz, t), ckv_pad, kr_pad,
                                    cache_ckv, cache_krt, page_table, tq=t, pages=64)
            x = _mla_out(x, _seq_to_heads(ol, tp), w, tm=rows)
        buf = state_conv[l]
        prev1 = jnp.pad(buf[:, CONV_W - 2:], ((0, 0), (0, tp - 1), (0, 0))).reshape(rows, FFN_DIM)
        prev2 = jnp.pad(buf, pad_tail).reshape(rows, FFN_DIM)
        x, u = _ffn_rows(x, prev1, prev2, ffn_w, l, tm=rows, tp=tp)
        bufs.append(u.reshape(bsz, tp, FFN_DIM)[:, t - (CONV_W - 1):t])
    return (x.reshape(bsz, tp, d)[:, :t], jnp.stack(states), jnp.stack(bufs), ckv[:, :t], kr[:, :t])


def kernel(x_prompt, x_sample, state_gla, state_ffn_conv, cache_ckv, cache_krope, page_table, norm_mix_pre, norm_mix_post, norm_ffn_pre, norm_ffn_post, ffn_w_gate, ffn_w_up, ffn_conv_w, ffn_conv_b, ffn_w_down, gla_w_q, gla_w_k, gla_w_v, gla_w_gk1, gla_w_gk2, gla_b_gk, gla_w_r, gla_norm, gla_w_o, mla_kv_src_norm, mla_w_dkv, mla_kv_norm, mla_w_kr, mla_w_uk, mla_w_uv, mla_w_dq, mla_q_norm, mla_w_uq, mla_w_o):
    p = dict(norm_mix_pre=norm_mix_pre, norm_mix_post=norm_mix_post, norm_ffn_pre=norm_ffn_pre,
             norm_ffn_post=norm_ffn_post, ffn_w_gate=ffn_w_gate, ffn_w_up=ffn_w_up, ffn_conv_w=ffn_conv_w,
             ffn_conv_b=ffn_conv_b, ffn_w_down=ffn_w_down, gla_w_q=gla_w_q, gla_w_k=gla_w_k, gla_w_v=gla_w_v,
             gla_w_gk1=gla_w_gk1, gla_w_gk2=gla_w_gk2, gla_b_gk=gla_b_gk, gla_w_r=gla_w_r, gla_norm=gla_norm,
             gla_w_o=gla_w_o, mla_kv_src_norm=mla_kv_src_norm, mla_w_dkv=mla_w_dkv, mla_kv_norm=mla_kv_norm,
             mla_w_kr=mla_w_kr, mla_w_uk=mla_w_uk, mla_w_uv=mla_w_uv, mla_w_dq=mla_w_dq, mla_q_norm=mla_q_norm,
             mla_w_uq=mla_w_uq, mla_w_o=mla_w_o)
    weights = _prep_weights(p)
    y_p, gla_p, conv_p, ckv_p, kr_p = _trunk_prompt(x_prompt, weights)
    y_s, gla_s, conv_s, ckv_s, kr_s = _trunk_sample(x_sample, state_gla, state_ffn_conv, cache_ckv,
                                                    cache_krope, page_table, weights)
    return (y_p, y_s, gla_p, gla_s, conv_p, conv_s, ckv_p, ckv_s, kr_p, kr_s)
```
